```python
import math, functools
import jax, jax.numpy as jnp
from jax import lax
import numpy as np

D_MODEL = 2048
BATCH = 4
SEQ = 2048
DEPTH = 4
DEC_BATCH = 8
DEC_SEQ = 1
PAST_LEN = 16384
PAGE_SIZE = 128

GLA_HEADS = 4
GLA_DK = D_MODEL // 16
GLA_DV = D_MODEL // 8
GLA_RANK = 16
GLA_TAU = 16.0
GLA_CHUNK = 64
ATT_HEADS = 8
KV_HEADS = 4
HEAD_DIM = 128
IDX_HEADS = 16
IDX_DIM = 64
TOPK_MAX = 256
Q_BLOCK = 128
ROPE_THETA = 500000.0
N_GROUPS = 4
EXP_PER_GROUP = 8
N_EXPERTS = N_GROUPS * EXP_PER_GROUP
D_EXPERT = 256
TOPK_INNER = 2
P_DIM = 256
LN_EPS = 1e-5

IN_SIZES = (
    GLA_HEADS * GLA_DK,
    GLA_HEADS * GLA_DK,
    GLA_HEADS * GLA_DV,
    GLA_HEADS * GLA_DV,
    GLA_RANK,
    ATT_HEADS * HEAD_DIM,
    KV_HEADS * HEAD_DIM,
    KV_HEADS * HEAD_DIM,
    IDX_HEADS * IDX_DIM,
    IDX_DIM,
    IDX_HEADS,
    D_MODEL,
    D_MODEL,
)
IN_WIDTH = sum(IN_SIZES)

kernel_name = "hybrid_gla_dsa_hmoe_decoder_step"


def layer_norm(x, g, b):
    xf = x.astype(jnp.float32)
    mu = jnp.mean(xf, -1, keepdims=True)
    var = jnp.mean(jnp.square(xf - mu), -1, keepdims=True)
    return ((xf - mu) * lax.rsqrt(var + LN_EPS) * g.astype(jnp.float32) + b.astype(jnp.float32)).astype(x.dtype)


def head_rms_norm(o, g):
    of = o.astype(jnp.float32)
    return (of * lax.rsqrt(jnp.mean(of * of, -1, keepdims=True) + LN_EPS) * g.astype(jnp.float32)).astype(o.dtype)


def rope_partial(x, pos):
    dh = x.shape[-1]
    rot = dh // 4
    half = rot // 2
    inv = ROPE_THETA ** (-jnp.arange(half, dtype=jnp.float32) / half)
    ang = pos.astype(jnp.float32)[:, None] * inv[None, :]
    cos = jnp.cos(ang)[:, None, :]
    sin = jnp.sin(ang)[:, None, :]
    x1 = x[..., :half].astype(jnp.float32)
    x2 = x[..., half:rot].astype(jnp.float32)
    r1 = (x1 * cos - x2 * sin).astype(x.dtype)
    r2 = (x2 * cos + x1 * sin).astype(x.dtype)
    return jnp.concatenate([r1, r2, x[..., rot:]], axis=-1)


def gla_chunked(q, k, v, log_a, s0):
    B, S, H, DK = q.shape
    DV = v.shape[-1]
    C = math.gcd(S, GLA_CHUNK)
    n = S // C
    f32 = jnp.float32

    def to_chunks(t):
        return t.reshape(B, n, C, H, t.shape[-1]).transpose(1, 0, 3, 2, 4)

    qc, kc, vc, ac = (to_chunks(t) for t in (q, k, v, log_a))
    causal = jnp.tril(jnp.ones((C, C), bool))

    def step(state, inp):
        qi, ki, vi, ai = (t.astype(f32) for t in inp)
        b = jnp.cumsum(ai, axis=2)
        diff = b[:, :, :, None, :] - b[:, :, None, :, :]
        decay = jnp.exp(jnp.where(causal[None, None, :, :, None], diff, -jnp.inf))
        att = jnp.einsum('bhtd,bhsd,bhtsd->bhts', qi, ki, decay)
        o = jnp.einsum('bhts,bhsv->bhtv', att, vi) + jnp.einsum('bhtd,bhdv->bhtv', qi * jnp.exp(b), state)
        b_last = b[:, :, -1:, :]
        k_dec = ki * jnp.exp(b_last - b)
        state = state * jnp.exp(b_last[:, :, 0, :, None]) + jnp.einsum('bhsd,bhsv->bhdv', k_dec, vi)
        return state, o

    s_fin, oc = lax.scan(step, s0.astype(f32), (qc, kc, vc, ac))
    o = oc.transpose(1, 0, 3, 2, 4).reshape(B, S, H, DV)
    return o.astype(q.dtype), s_fin.astype(s0.dtype)


def indexer_scores(qi, wi, ki):
    s = jnp.einsum('bqhd,bkd->bqhk', qi, ki).astype(jnp.float32)
    return jnp.einsum('bqh,bqhk->bqk', wi.astype(jnp.float32), jax.nn.relu(s))


def sparse_attend(q, k_sel, v_sel, valid):
    B, T, H, Dh = q.shape
    G = H // KV_HEADS
    qg = q.reshape(B, T, KV_HEADS, G, Dh)
    s = jnp.einsum('btngd,btknd->btngk', qg, k_sel).astype(jnp.float32) * (Dh ** -0.5)
    s = jnp.where(valid[:, :, None, None, :], s, -jnp.inf)
    p = jax.nn.softmax(s, axis=-1)
    o = jnp.einsum('btngk,btknd->btngd', p.astype(v_sel.dtype), v_sel)
    return o.reshape(B, T, H * Dh)


_gather_rows = jax.vmap(lambda rows, idx: rows[idx])


def dsa_prompt(q, k, v, qi, wi, ki):
    B, S = q.shape[:2]
    topk = min(TOPK_MAX, S // 4)
    qb_len = math.gcd(S, Q_BLOCK)
    key_pos = jnp.arange(S)

    def block(i):
        t0 = i * qb_len
        qb = lax.dynamic_slice_in_dim(q, t0, qb_len, axis=1)
        qib = lax.dynamic_slice_in_dim(qi, t0, qb_len, axis=1)
        wib = lax.dynamic_slice_in_dim(wi, t0, qb_len, axis=1)
        qpos = t0 + jnp.arange(qb_len)
        scores = indexer_scores(qib, wib, ki)
        vis = key_pos[None, :] <= qpos[:, None]
        scores = jnp.where(vis[None], scores, -jnp.inf)
        _, idx = lax.top_k(scores, topk)
        valid = idx <= qpos[None, :, None]
        return sparse_attend(qb, _gather_rows(k, idx), _gather_rows(v, idx), valid)

    out = lax.map(block, jnp.arange(S // qb_len))
    return out.transpose(1, 0, 2, 3).reshape(B, S, -1)


def dsa_sample(q, k_new, v_new, qi, wi, ki_new, cache_k, cache_v, cache_kidx, layer, page_table):
    DB, DS = q.shape[:2]
    n_past = page_table.shape[1] * PAGE_SIZE
    L = n_past + DS
    topk = min(TOPK_MAX, L // 4)
    ki_past = cache_kidx[layer, page_table].reshape(DB, n_past, IDX_DIM)
    ki_all = jnp.concatenate([ki_past, ki_new.astype(ki_past.dtype)], axis=1)
    scores = indexer_scores(qi, wi, ki_all)
    qpos = n_past + jnp.arange(DS)
    vis = jnp.arange(L)[None, :] <= qpos[:, None]
    scores = jnp.where(vis[None], scores, -jnp.inf)
    _, idx = lax.top_k(scores, topk)
    valid = idx <= qpos[None, :, None]
    past_i = jnp.minimum(idx, n_past - 1)
    phys = jnp.take_along_axis(page_table, (past_i // PAGE_SIZE).reshape(DB, -1), axis=1).reshape(idx.shape)
    off = past_i % PAGE_SIZE
    new_i = jnp.clip(idx - n_past, 0, DS - 1)
    is_past = (idx < n_past)[..., None, None]
    k_sel = jnp.where(is_past, cache_k[layer, phys, off], _gather_rows(k_new, new_i).astype(cache_k.dtype))
    v_sel = jnp.where(is_past, cache_v[layer, phys, off], _gather_rows(v_new, new_i).astype(cache_v.dtype))
    return sparse_attend(q, k_sel, v_sel, valid)


def token_mixer(x, pos, s0, dsa_fn, w_in_l, w_decay_l, b_decay_l, gla_norm_g_l, w_bg_l, w_bd_l, w_out_l):
    B, S, _ = x.shape
    h = x @ w_in_l
    points = [int(p) for p in np.cumsum(IN_SIZES)[:-1]]
    (gq, gk, gv, gr, ga, aq, ak, av, iq, ik, iw, gate_g, gate_d) = jnp.split(h, points, axis=-1)
    q = gq.reshape(B, S, GLA_HEADS, GLA_DK) * (GLA_DK ** -0.5)
    k = gk.reshape(B, S, GLA_HEADS, GLA_DK)
    v = gv.reshape(B, S, GLA_HEADS, GLA_DV)
    log_a = jax.nn.log_sigmoid((ga @ w_decay_l + b_decay_l).astype(jnp.float32)).reshape(B, S, GLA_HEADS, GLA_DK) / GLA_TAU
    o, s_fin = gla_chunked(q, k, v, log_a, s0)
    o = head_rms_norm(o, gla_norm_g_l).reshape(B, S, GLA_HEADS * GLA_DV) * jax.nn.silu(gr)
    branch_g = o @ w_bg_l
    qa = rope_partial(aq.reshape(B, S, ATT_HEADS, HEAD_DIM), pos)
    ka = rope_partial(ak.reshape(B, S, KV_HEADS, HEAD_DIM), pos)
    va = av.reshape(B, S, KV_HEADS, HEAD_DIM)
    qi = rope_partial(iq.reshape(B, S, IDX_HEADS, IDX_DIM), pos) * (IDX_DIM ** -0.5)
    ki = rope_partial(ik.reshape(B, S, 1, IDX_DIM), pos)[:, :, 0]
    wi = iw * (IDX_HEADS ** -0.5)
    att = dsa_fn(qa, ka, va, qi, wi, ki)
    branch_d = att @ w_bd_l
    merged = jax.nn.sigmoid(gate_g) * branch_g + jax.nn.sigmoid(gate_d) * branch_d
    return merged @ w_out_l, s_fin, ka, va, ki


def hier_moe(x, w_group_l, b_group_l, w_router_l, b_router_l, w_gate_l, w_up_l, w_down_l):
    T = x.shape[0]
    pg = jax.nn.softmax((x @ w_group_l + b_group_l).astype(jnp.float32), axis=-1)
    g_top = jnp.argmax(pg, axis=-1)
    pg_top = jnp.max(pg, axis=-1)
    le = (x @ w_router_l + b_router_l).astype(jnp.float32).reshape(T, N_GROUPS, EXP_PER_GROUP)
    le_sel = jnp.take_along_axis(le, g_top[:, None, None], axis=1)[:, 0]
    pe = jax.nn.softmax(le_sel, axis=-1)
    v2, i2 = lax.top_k(pe, TOPK_INNER)
    w2 = pg_top[:, None] * v2 / jnp.sum(v2, -1, keepdims=True)
    expert_id = g_top[:, None] * EXP_PER_GROUP + i2
    gate = jnp.sum(jax.nn.one_hot(expert_id, N_EXPERTS, dtype=jnp.float32) * w2[..., None], axis=1)
    hid = jax.nn.silu(jnp.einsum('td,edf->tef', x, w_gate_l)) * jnp.einsum('td,edf->tef', x, w_up_l)
    return jnp.einsum('tef,efd->td', hid * gate[:, :, None].astype(hid.dtype), w_down_l)


def run_trunk(x, p, pos, gla_s0, dsa_fn, weights):
    (ln_in_g, ln_in_b, w_in, w_decay, b_decay, gla_norm_g, w_branch_gla, w_branch_dsa, w_out,
     ln_mix_g, ln_mix_b, w_group, b_group, w_router, b_router, w_gate, w_up, w_down,
     ln_moe_g, ln_moe_b, w_ple, w_ple_gate, ln_ple_g, ln_ple_b) = weights
    alpha = (2.0 * DEPTH) ** 0.25
    B, S, D = x.shape
    x = layer_norm(x, ln_in_g, ln_in_b)
    ks, vs, kis, ss = [], [], [], []
    for l in range(DEPTH):
        mix, s_fin, k, v, ki = token_mixer(x, pos, gla_s0[l], functools.partial(dsa_fn, l), w_in[l], w_decay[l],
                                           b_decay[l], gla_norm_g[l], w_branch_gla[l], w_branch_dsa[l], w_out[l])
        x = layer_norm(alpha * x + mix, ln_mix_g[l], ln_mix_b[l])
        ffn = hier_moe(x.reshape(B * S, D), w_group[l], b_group[l], w_router[l], b_router[l],
                       w_gate[l], w_up[l], w_down[l]).reshape(B, S, D)
        x = layer_norm(alpha * x + ffn, ln_moe_g[l], ln_moe_b[l])
        ple = (p[l] @ w_ple[l]) * jax.nn.sigmoid(x @ w_ple_gate[l])
        x = layer_norm(alpha * x + ple, ln_ple_g[l], ln_ple_b[l])
        ks.append(k)
        vs.append(v)
        kis.append(ki)
        ss.append(s_fin)
    return x, jnp.stack(ks), jnp.stack(vs), jnp.stack(kis), jnp.stack(ss)


def setup_inputs(seed: int = 0) -> dict:
    key = jax.random.key(seed)
    ks = list(jax.random.split(key, 48))
    f32 = jnp.float32
    beta = (8.0 * DEPTH) ** -0.25
    n_pages = PAST_LEN // PAGE_SIZE
    n_used = DEC_BATCH * n_pages
    n_pool = n_used + max(1, n_used // 4)

    def nrm(shape, scale=1.0):
        return jax.random.normal(ks.pop(), shape, f32) * scale

    def gain(shape):
        return 1.0 + nrm(shape, 0.02)

    page_table = jax.random.permutation(ks.pop(), n_pool)[:n_used].reshape(DEC_BATCH, n_pages).astype(jnp.int32)
    gla_v = GLA_HEADS * GLA_DV
    att_w = ATT_HEADS * HEAD_DIM
    return {
        "x_prompt": nrm((BATCH, SEQ, D_MODEL)),
        "x_sample": nrm((DEC_BATCH, DEC_SEQ, D_MODEL)),
        "cache_k": nrm((DEPTH, n_pool, PAGE_SIZE, KV_HEADS, HEAD_DIM)),
        "cache_v": nrm((DEPTH, n_pool, PAGE_SIZE, KV_HEADS, HEAD_DIM)),
        "cache_kidx": nrm((DEPTH, n_pool, PAGE_SIZE, IDX_DIM)),
        "state_gla": nrm((DEPTH, DEC_BATCH, GLA_HEADS, GLA_DK, GLA_DV)),
        "page_table": page_table,
        "p_prompt": nrm((DEPTH, BATCH, SEQ, P_DIM)),
        "p_sample": nrm((DEPTH, DEC_BATCH, DEC_SEQ, P_DIM)),
        "ln_in_g": gain((D_MODEL,)),
        "ln_in_b": nrm((D_MODEL,), 0.02),
        "w_in": nrm((DEPTH, D_MODEL, IN_WIDTH), D_MODEL ** -0.5),
        "w_decay": nrm((DEPTH, GLA_RANK, GLA_HEADS * GLA_DK), GLA_RANK ** -0.5),
        "b_decay": nrm((DEPTH, GLA_HEADS * GLA_DK), 0.1),
        "gla_norm_g": gain((DEPTH, GLA_DV)),
        "w_branch_gla": nrm((DEPTH, gla_v, D_MODEL), beta * gla_v ** -0.5),
        "w_branch_dsa": nrm((DEPTH, att_w, D_MODEL), beta * att_w ** -0.5),
        "w_out": nrm((DEPTH, D_MODEL, D_MODEL), beta * D_MODEL ** -0.5),
        "ln_mix_g": gain((DEPTH, D_MODEL)),
        "ln_mix_b": nrm((DEPTH, D_MODEL), 0.02),
        "w_group": nrm((DEPTH, D_MODEL, N_GROUPS), D_MODEL ** -0.5),
        "b_group": nrm((DEPTH, N_GROUPS), 0.01),
        "w_router": nrm((DEPTH, D_MODEL, N_EXPERTS), D_MODEL ** -0.5),
        "b_router": nrm((DEPTH, N_EXPERTS), 0.01),
        "w_gate": nrm((DEPTH, N_EXPERTS, D_MODEL, D_EXPERT), D_MODEL ** -0.5),
        "w_up": nrm((DEPTH, N_EXPERTS, D_MODEL, D_EXPERT), D_MODEL ** -0.5),
        "w_down": nrm((DEPTH, N_EXPERTS, D_EXPERT, D_MODEL), beta * D_EXPERT ** -0.5),
        "ln_moe_g": gain((DEPTH, D_MODEL)),
        "ln_moe_b": nrm((DEPTH, D_MODEL), 0.02),
        "w_ple": nrm((DEPTH, P_DIM, D_MODEL), beta * P_DIM ** -0.5),
        "w_ple_gate": nrm((DEPTH, D_MODEL, D_MODEL), D_MODEL ** -0.5),
        "ln_ple_g": gain((DEPTH, D_MODEL)),
        "ln_ple_b": nrm((DEPTH, D_MODEL), 0.02),
    }


def reference(x_prompt, x_sample, cache_k, cache_v, cache_kidx, state_gla, page_table, p_prompt, p_sample,
              ln_in_g, ln_in_b, w_in, w_decay, b_decay, gla_norm_g, w_branch_gla, w_branch_dsa, w_out,
              ln_mix_g, ln_mix_b, w_group, b_group, w_router, b_router, w_gate, w_up, w_down,
              ln_moe_g, ln_moe_b, w_ple, w_ple_gate, ln_ple_g, ln_ple_b):
    weights = (ln_in_g, ln_in_b, w_in, w_decay, b_decay, gla_norm_g, w_branch_gla, w_branch_dsa, w_out,
               ln_mix_g, ln_mix_b, w_group, b_group, w_router, b_router, w_gate, w_up, w_down,
               ln_moe_g, ln_moe_b, w_ple, w_ple_gate, ln_ple_g, ln_ple_b)
    n_past = page_table.shape[1] * PAGE_SIZE

    def dsa_prompt_fn(layer, q, k, v, qi, wi, ki):
        return dsa_prompt(q, k, v, qi, wi, ki)

    def dsa_sample_fn(layer, q, k, v, qi, wi, ki):
        return dsa_sample(q, k, v, qi, wi, ki, cache_k, cache_v, cache_kidx, layer, page_table)

    s0_prompt = jnp.zeros((DEPTH, x_prompt.shape[0], GLA_HEADS, GLA_DK, GLA_DV), state_gla.dtype)
    y_prompt, k_prompt, v_prompt, kidx_prompt, gla_prompt = run_trunk(
        x_prompt, p_prompt, jnp.arange(x_prompt.shape[1]), s0_prompt, dsa_prompt_fn, weights)
    y_sample, k_sample, v_sample, kidx_sample, gla_sample = run_trunk(
        x_sample, p_sample, n_past + jnp.arange(x_sample.shape[1]), state_gla, dsa_sample_fn, weights)
    return (y_prompt, y_sample, k_prompt, v_prompt, kidx_prompt, gla_prompt,
            k_sample, v_sample, kidx_sample, gla_sample)
```

```python
import functools
import math

import numpy as np
import jax
import jax.numpy as jnp
from jax import lax
from jax.experimental import pallas as pl
from jax.experimental.pallas import tpu as pltpu

F32 = jnp.float32
BF16 = jnp.bfloat16
I32 = jnp.int32

GLA_HEADS = 4
GLA_DK = 128
GLA_DV = 256
GLA_RANK = 16
GLA_TAU = 16.0
ATT_HEADS = 8
KV_HEADS = 4
HEAD_DIM = 128
IDX_HEADS = 16
IDX_DIM = 64
TOPK_MAX = 256
ROPE_THETA = 500000.0
N_GROUPS = 4
EXP_PER_GROUP = 8
N_EXPERTS = N_GROUPS * EXP_PER_GROUP
TOPK_INNER = 2
LN_EPS = 1e-5

LANES = 128
SUBLANES = 8
TM_IN = 768
TN_IN = 1024
TM = 256
GLA_CHUNK = 128
TQ = 128
VMEM_LIMIT = 56 * 1024 * 1024
INT_MIN = -(2 ** 31)

C_GQ, C_GK, C_GV, C_GR = 0, 512, 1024, 2048
C_AQ, C_AK, C_AV, C_IQ = 3072, 4096, 4608, 5120
C_GG, C_GD = 6144, 8192
N_MAIN = 10240
L_IK, L_GA, L_IW = 0, 64, 80


def _cparams(sem):
    return pltpu.CompilerParams(dimension_semantics=sem, vmem_limit_bytes=VMEM_LIMIT)


def _bdot(a, b):
    return jnp.dot(a.astype(BF16), b.astype(BF16), preferred_element_type=F32)


def _bdot_nt(a, b):
    return lax.dot_general(a.astype(BF16), b.astype(BF16), (((1,), (1,)), ((), ())),
                           preferred_element_type=F32)


def _split3(x):
    hi = x.astype(BF16)
    r = x - hi.astype(F32)
    mid = r.astype(BF16)
    lo = (r - mid.astype(F32)).astype(BF16)
    return hi, mid, lo


def _ln(x, g, b):
    mu = jnp.mean(x, axis=-1, keepdims=True)
    xc = x - mu
    var = jnp.mean(xc * xc, axis=-1, keepdims=True)
    return xc * lax.rsqrt(var + LN_EPS) * g + b


def _col_to_row(col, n):
    eye = lax.broadcasted_iota(I32, (n, n), 0) == lax.broadcasted_iota(I32, (n, n), 1)
    return jnp.sum(jnp.where(eye, col, 0.0), axis=0, keepdims=True)


def _row_to_col(row, n):
    eye = lax.broadcasted_iota(I32, (n, n), 0) == lax.broadcasted_iota(I32, (n, n), 1)
    return jnp.sum(jnp.where(eye, row, 0.0), axis=1, keepdims=True)


def _sort_key(x):
    bits = lax.bitcast_convert_type(x, I32)
    return jnp.where(bits < 0, bits ^ jnp.int32(0x7FFFFFFF), bits)


def _ln_in_kernel(x_ref, g_ref, b_ref, o_ref, ob_ref):
    y = _ln(x_ref[...], g_ref[...], b_ref[...])
    o_ref[...] = y
    ob_ref[...] = y.astype(BF16)


def ln_in(x, g, b):
    T, D = x.shape
    return pl.pallas_call(
        _ln_in_kernel,
        grid=(T // TM,),
        in_specs=[pl.BlockSpec((TM, D), lambda i: (i, 0)),
                  pl.BlockSpec((1, D), lambda i: (0, 0)),
                  pl.BlockSpec((1, D), lambda i: (0, 0))],
        out_specs=[pl.BlockSpec((TM, D), lambda i: (i, 0)),
                   pl.BlockSpec((TM, D), lambda i: (i, 0))],
        out_shape=[jax.ShapeDtypeStruct((T, D), F32), jax.ShapeDtypeStruct((T, D), BF16)],
        compiler_params=_cparams(("parallel",)),
        name="ln_in",
    )(x, g.reshape(1, D), b.reshape(1, D))


def _mm_kernel(x_ref, w_ref, o_ref):
    o_ref[...] = jnp.dot(x_ref[...], w_ref[...], preferred_element_type=F32)


def in_proj(xb, w, layer, tn):
    T, D = xb.shape
    N = w.shape[-1]
    return pl.pallas_call(
        _mm_kernel,
        grid=(N // tn, T // TM_IN),
        in_specs=[pl.BlockSpec((TM_IN, D), lambda j, i: (i, 0)),
                  pl.BlockSpec((None, D, tn), lambda j, i: (layer, 0, j))],
        out_specs=pl.BlockSpec((TM_IN, tn), lambda j, i: (i, j)),
        out_shape=jax.ShapeDtypeStruct((T, N), F32),
        compiler_params=_cparams(("parallel", "parallel")),
        name="in_proj",
    )(xb, w)


def _rope_tile(x, c, s1, s2, shift):
    w = x.shape[-1]
    xm = pltpu.roll(x, w - shift, axis=1)
    xp = pltpu.roll(x, shift, axis=1)
    return x * c + xm * s1 + xp * s2


def _rope_kernel(aq_ref, ak_ref, av_ref, iq_ref, hs_ref, t128_ref, t64_ref,
                 ka_ref, va_ref, ki_ref, qab_ref, kab_ref, vab_ref, qib_ref, kie_ref, kio_ref):
    c1, s1a, s1b = t128_ref[0], t128_ref[1], t128_ref[2]
    c2, s2a, s2b = t64_ref[0], t64_ref[1], t64_ref[2]
    for hd in range(ATT_HEADS):
        sl = slice(hd * HEAD_DIM, (hd + 1) * HEAD_DIM)
        qab_ref[:, sl] = _rope_tile(aq_ref[:, sl], c1, s1a, s1b, HEAD_DIM // 8).astype(BF16)
    for hd in range(KV_HEADS):
        sl = slice(hd * HEAD_DIM, (hd + 1) * HEAD_DIM)
        kr = _rope_tile(ak_ref[:, sl], c1, s1a, s1b, HEAD_DIM // 8)
        ka_ref[:, sl] = kr
        kab_ref[:, sl] = kr.astype(BF16)
    v = av_ref[...]
    va_ref[...] = v
    vab_ref[...] = v.astype(BF16)
    for pr in range(IDX_HEADS * IDX_DIM // LANES):
        sl = slice(pr * LANES, (pr + 1) * LANES)
        qr = _rope_tile(iq_ref[:, sl], c2, s2a, s2b, IDX_DIM // 8) * (IDX_DIM ** -0.5)
        qib_ref[:, sl] = qr.astype(BF16)
    kir = _rope_tile(hs_ref[...], c2, s2a, s2b, IDX_DIM // 8)
    ki_ref[...] = kir[:, :IDX_DIM]
    lane = lax.broadcasted_iota(I32, kir.shape, 1)
    ke = jnp.where(lane < IDX_DIM, kir, 0.0)
    kie_ref[...] = ke.astype(BF16)
    kio_ref[...] = pltpu.roll(ke, IDX_DIM, axis=1).astype(BF16)


def rope_stage(h, hs, t128, t64):
    T = h.shape[0]
    aw, kw, iw = ATT_HEADS * HEAD_DIM, KV_HEADS * HEAD_DIM, IDX_HEADS * IDX_DIM
    row = lambda w, c: pl.BlockSpec((TM, w), lambda i: (i, c))
    tab = pl.BlockSpec((3, TM, LANES), lambda i: (0, i, 0))
    return pl.pallas_call(
        _rope_kernel,
        grid=(T // TM,),
        in_specs=[row(aw, C_AQ // aw), row(kw, C_AK // kw), row(kw, C_AV // kw), row(iw, C_IQ // iw),
                  row(LANES, 0), tab, tab],
        out_specs=[row(kw, 0), row(kw, 0), row(IDX_DIM, 0), row(aw, 0), row(kw, 0), row(kw, 0),
                   row(iw, 0), row(LANES, 0), row(LANES, 0)],
        out_shape=[jax.ShapeDtypeStruct((T, kw), F32), jax.ShapeDtypeStruct((T, kw), F32),
                   jax.ShapeDtypeStruct((T, IDX_DIM), F32), jax.ShapeDtypeStruct((T, aw), BF16),
                   jax.ShapeDtypeStruct((T, kw), BF16), jax.ShapeDtypeStruct((T, kw), BF16),
                   jax.ShapeDtypeStruct((T, iw), BF16), jax.ShapeDtypeStruct((T, LANES), BF16),
                   jax.ShapeDtypeStruct((T, LANES), BF16)],
        compiler_params=_cparams(("parallel",)),
        name="rope_stage",
    )(h, h, h, h, hs, t128, t64)


def _log_decay(hs, wdp_ref, bd_ref):
    z = _bdot(hs, wdp_ref[...]) + bd_ref[...]
    return (jnp.minimum(z, 0.0) - jnp.log1p(jnp.exp(-jnp.abs(z)))) * (1.0 / GLA_TAU)


def _gla_out(o, gn, gr):
    on = o * lax.rsqrt(jnp.mean(o * o, axis=-1, keepdims=True) + LN_EPS) * gn
    return on * (gr * jax.nn.sigmoid(gr))


def _gla_prompt_kernel(hq_ref, hk_ref, hv_ref, hr_ref, hs_ref, wdp_ref, bd_ref, gn_ref,
                       og_ref, sfin_ref, s_ref):
    c = pl.program_id(1)
    C = GLA_CHUNK

    @pl.when(c == 0)
    def _():
        s_ref[...] = jnp.zeros_like(s_ref)

    la_all = _log_decay(hs_ref[...], wdp_ref, bd_ref)
    r_i = lax.broadcasted_iota(I32, (C, C), 0)
    c_i = lax.broadcasted_iota(I32, (C, C), 1)
    causal = r_i >= c_i
    tri = jnp.where(causal, 1.0, 0.0).astype(BF16)
    gn = gn_ref[...]
    for hd in range(GLA_HEADS):
        ks = slice(hd * GLA_DK, (hd + 1) * GLA_DK)
        vs = slice(hd * GLA_DV, (hd + 1) * GLA_DV)
        la = la_all[:, ks]
        hi, mid, lo = _split3(la)
        b = (jnp.dot(tri, hi, preferred_element_type=F32)
             + jnp.dot(tri, mid, preferred_element_type=F32)
             + jnp.dot(tri, lo, preferred_element_type=F32))
        eb = jnp.exp(b)
        qe = hq_ref[:, ks] * (GLA_DK ** -0.5) * eb
        ke = hk_ref[:, ks] * jnp.exp(-b)
        v = hv_ref[:, vs]
        s_old = s_ref[hd]
        att = jnp.where(causal, _bdot_nt(qe, ke), 0.0)
        o = _bdot(att, v) + _bdot(qe, s_old)
        og_ref[:, vs] = _gla_out(o, gn, hr_ref[:, vs]).astype(BF16)
        eb_last = eb[C - 1:C, :]
        kdec = ke * eb_last
        upd = lax.dot_general(kdec.astype(BF16), v.astype(BF16), (((0,), (0,)), ((), ())),
                              preferred_element_type=F32)
        s_new = s_old * _row_to_col(jnp.broadcast_to(eb_last, (GLA_DK, GLA_DK)), GLA_DK) + upd
        s_ref[hd] = s_new

        @pl.when(c == pl.num_programs(1) - 1)
        def _():
            sfin_ref[0, hd] = s_new


def gla_prompt(h, hs, wdp, bd, gn, B, S):
    nC = S // GLA_CHUNK
    kw, vw = GLA_HEADS * GLA_DK, GLA_HEADS * GLA_DV
    row = lambda w, col: pl.BlockSpec((GLA_CHUNK, w), lambda b, c: (b * nC + c, col))
    cst = lambda shp: pl.BlockSpec(shp, lambda b, c: (0,) * len(shp))
    return pl.pallas_call(
        _gla_prompt_kernel,
        grid=(B, nC),
        in_specs=[row(kw, C_GQ // kw), row(kw, C_GK // kw), row(vw, C_GV // vw), row(vw, C_GR // vw),
                  row(LANES, 0), cst((LANES, kw)), cst((1, kw)), cst((1, GLA_DV))],
        out_specs=[row(vw, 0),
                   pl.BlockSpec((1, GLA_HEADS, GLA_DK, GLA_DV), lambda b, c: (b, 0, 0, 0))],
        out_shape=[jax.ShapeDtypeStruct((B * S, vw), BF16),
                   jax.ShapeDtypeStruct((B, GLA_HEADS, GLA_DK, GLA_DV), F32)],
        scratch_shapes=[pltpu.VMEM((GLA_HEADS, GLA_DK, GLA_DV), F32)],
        compiler_params=_cparams(("arbitrary", "arbitrary")),
        name="gla_prompt",
    )(h, h, h, h, hs, wdp, bd, gn)


def _gla_sample_kernel(hq_ref, hk_ref, hv_ref, hr_ref, hs_ref, wdp_ref, bd_ref, gn_ref, s0_ref,
                       og_ref, s1_ref):
    b = pl.program_id(0)
    la_all = _log_decay(hs_ref[...], wdp_ref, bd_ref)
    gn = gn_ref[...]
    rowsel = lax.broadcasted_iota(I32, (hq_ref.shape[0], 1), 0) == b

    def pick(x):
        return jnp.sum(jnp.where(rowsel, x, 0.0), axis=0, keepdims=True)

    for hd in range(GLA_HEADS):
        ks = slice(hd * GLA_DK, (hd + 1) * GLA_DK)
        vs = slice(hd * GLA_DV, (hd + 1) * GLA_DV)
        a_row = jnp.exp(pick(la_all[:, ks]))
        q_row = pick(hq_ref[:, ks]) * (GLA_DK ** -0.5)
        k_row = pick(hk_ref[:, ks])
        v_row = pick(hv_ref[:, vs])
        bc = lambda r: _row_to_col(jnp.broadcast_to(r, (GLA_DK, GLA_DK)), GLA_DK)
        s_new = s0_ref[0, hd] * bc(a_row) + bc(k_row) * v_row
        s1_ref[0, hd] = s_new
        o = jnp.sum(bc(q_row) * s_new, axis=0, keepdims=True)
        og_ref[pl.ds(b, 1), vs] = _gla_out(o, gn, pick(hr_ref[:, vs]))


def gla_sample(h, hs, wdp, bd, gn, state, layer, row0, DB):
    kw, vw = GLA_HEADS * GLA_DK, GLA_HEADS * GLA_DV
    rb = row0 // DB
    row = lambda w, col: pl.BlockSpec((DB, w), lambda b: (rb, col))
    cst = lambda shp: pl.BlockSpec(shp, lambda b: (0,) * len(shp))
    return pl.pallas_call(
        _gla_sample_kernel,
        grid=(DB,),
        in_specs=[row(kw, C_GQ // kw), row(kw, C_GK // kw), row(vw, C_GV // vw), row(vw, C_GR // vw),
                  row(LANES, 0), cst((LANES, kw)), cst((1, kw)), cst((1, GLA_DV)),
                  pl.BlockSpec((None, 1, GLA_HEADS, GLA_DK, GLA_DV), lambda b: (layer, b, 0, 0, 0))],
        out_specs=[pl.BlockSpec((DB, vw), lambda b: (0, 0)),
                   pl.BlockSpec((1, GLA_HEADS, GLA_DK, GLA_DV), lambda b: (b, 0, 0, 0))],
        out_shape=[jax.ShapeDtypeStruct((DB, vw), F32),
                   jax.ShapeDtypeStruct((DB, GLA_HEADS, GLA_DK, GLA_DV), F32)],
        compiler_params=_cparams(("arbitrary",)),
        name="gla_sample",
    )(h, h, h, h, hs, wdp, bd, gn, state)


def _kth_threshold(key, extra_key, topk):
    R = key.shape[0]

    def body(i, t):
        cand = t + jnp.left_shift(jnp.int32(1), 31 - i)
        cnt = jnp.sum(jnp.where(key >= cand, 1.0, 0.0), axis=1, keepdims=True)
        if extra_key is not None:
            cnt = cnt + jnp.where(extra_key >= cand, 1.0, 0.0)
        return jnp.where(cnt >= float(topk), cand, t)

    return lax.fori_loop(0, 32, body, jnp.full((R, 1), INT_MIN, I32))


def _dsa_prompt_kernel(qa_ref, qi_ref, hs_ref, ka_ref, va_ref, kie_ref, kio_ref, o_ref, *, topk):
    qb = pl.program_id(1)
    S = ka_ref.shape[0]
    hs = hs_ref[...]
    acc = jnp.zeros((TQ, S), F32)
    for hd in range(IDX_HEADS):
        pr = hd // 2
        q = qi_ref[:, pr * LANES:(pr + 1) * LANES]
        kk = kie_ref[...] if hd % 2 == 0 else kio_ref[...]
        s = lax.dot_general(q, kk, (((1,), (1,)), ((), ())), preferred_element_type=F32)
        w = hs[:, L_IW + hd:L_IW + hd + 1] * (IDX_HEADS ** -0.5)
        acc = acc + w * jnp.maximum(s, 0.0)
    qpos = qb * TQ + lax.broadcasted_iota(I32, (TQ, S), 0)
    kpos = lax.broadcasted_iota(I32, (TQ, S), 1)
    vis = kpos <= qpos
    key = jnp.where(vis, _sort_key(acc), INT_MIN)
    thr = _kth_threshold(key, None, topk)
    sel = jnp.logical_and(key >= thr, vis)
    G = ATT_HEADS // KV_HEADS
    for n in range(KV_HEADS):
        ksl = slice(n * HEAD_DIM, (n + 1) * HEAD_DIM)
        kn = ka_ref[:, ksl]
        vn = va_ref[:, ksl]
        for g in range(G):
            hsl = slice((n * G + g) * HEAD_DIM, (n * G + g + 1) * HEAD_DIM)
            s = lax.dot_general(qa_ref[:, hsl], kn, (((1,), (1,)), ((), ())),
                                preferred_element_type=F32) * (HEAD_DIM ** -0.5)
            s = jnp.where(sel, s, -jnp.inf)
            m = jnp.max(s, axis=1, keepdims=True)
            p = jnp.exp(s - m)
            l = jnp.sum(p, axis=1, keepdims=True)
            o = jnp.dot(p.astype(BF16), vn, preferred_element_type=F32) / l
            o_ref[:, hsl] = o.astype(BF16)


def dsa_prompt(qab, qib, hs, kab, vab, kie, kio, B, S):
    nq = S // TQ
    aw, kw, iw = ATT_HEADS * HEAD_DIM, KV_HEADS * HEAD_DIM, IDX_HEADS * IDX_DIM
    topk = min(TOPK_MAX, S // 4)
    qrow = lambda w: pl.BlockSpec((TQ, w), lambda b, q: (b * nq + q, 0))
    seq = lambda w: pl.BlockSpec((S, w), lambda b, q: (b, 0))
    return pl.pallas_call(
        functools.partial(_dsa_prompt_kernel, topk=topk),
        grid=(B, nq),
        in_specs=[qrow(aw), qrow(iw), qrow(LANES), seq(kw), seq(kw), seq(LANES), seq(LANES)],
        out_specs=qrow(aw),
        out_shape=jax.ShapeDtypeStruct((B * S, aw), BF16),
        compiler_params=_cparams(("parallel", "parallel")),
        name="dsa_prompt",
    )(qab, qib, hs, kab, vab, kie, kio)


def _dsa_sample_kernel(pt_ref, qi_ref, w_ref, kin_ref, qa_ref, kn_ref, vn_ref,
                       ckidx_hbm, ck_hbm, cv_hbm, o_ref,
                       kbuf, sc_ref, idxv_ref, idxs_ref, kg_ref, vg_ref, sem_i, sem_x, sem_k, sem_v,
                       *, layer, n_pages, page, topk):
    b = pl.program_id(0)
    NP, PG, K = n_pages, page, topk

    def icopy(p):
        return pltpu.make_async_copy(ckidx_hbm.at[layer, pt_ref[b, p]], kbuf.at[p], sem_i)

    lax.fori_loop(0, NP, lambda p, c: (icopy(p).start(), c)[1], 0)
    lax.fori_loop(0, NP, lambda p, c: (icopy(p).wait(), c)[1], 0)

    q = qi_ref[0]
    wcol = w_ref[0]

    def score_page(p, c):
        s = lax.dot_general(q, kbuf[p].astype(BF16), (((1,), (1,)), ((), ())),
                            preferred_element_type=F32)
        sc_ref[pl.ds(p, 1), :] = jnp.sum(wcol * jnp.maximum(s, 0.0), axis=0, keepdims=True)
        return c

    lax.fori_loop(0, NP, score_page, 0)
    s_new = jnp.sum(q.astype(F32) * kin_ref[0].astype(F32), axis=1, keepdims=True)
    s_new = jnp.sum(wcol * jnp.maximum(s_new, 0.0), axis=0, keepdims=True)

    key = _sort_key(sc_ref[...])
    key_new = _sort_key(s_new)

    def body(i, t):
        cand = t + jnp.left_shift(jnp.int32(1), 31 - i)
        cnt = jnp.sum(jnp.sum(jnp.where(key >= cand, 1.0, 0.0), axis=1, keepdims=True), axis=0, keepdims=True)
        cnt = cnt + jnp.where(key_new >= cand, 1.0, 0.0)
        return jnp.where(cnt >= float(K), cand, t)

    thr = lax.fori_loop(0, 32, body, jnp.full((1, 1), INT_MIN, I32))
    gt = jnp.where(key > thr, 1.0, 0.0)
    eq = jnp.where(key == thr, 1.0, 0.0)
    n_gt = (jnp.sum(jnp.sum(gt, axis=1, keepdims=True), axis=0, keepdims=True)
            + jnp.where(key_new > thr, 1.0, 0.0))
    need = float(K) - n_gt

    r_p = lax.broadcasted_iota(I32, (PG, PG), 0)
    c_p = lax.broadcasted_iota(I32, (PG, PG), 1)
    ut = jnp.where(r_p <= c_p, 1.0, 0.0).astype(BF16)
    r_n = lax.broadcasted_iota(I32, (NP, NP), 0)
    c_n = lax.broadcasted_iota(I32, (NP, NP), 1)
    slt = jnp.where(c_n < r_n, 1.0, 0.0).astype(BF16)

    def prefix(m):
        cs = jnp.dot(m.astype(BF16), ut, preferred_element_type=F32)
        tot = cs[:, PG - 1:PG]
        off = jnp.dot(slt, jnp.broadcast_to(tot, (NP, PG)).astype(BF16), preferred_element_type=F32)[:, 0:1]
        return cs, off, tot

    cs_e, off_e, tot_e = prefix(eq)
    tie_rank = off_e + cs_e - eq
    sel = jnp.maximum(gt, jnp.where(tie_rank < need, eq, 0.0))
    n_eq_past = jnp.sum(tot_e, axis=0, keepdims=True)
    sel_new = jnp.logical_or(key_new > thr, jnp.logical_and(key_new == thr, n_eq_past < need))

    cs, off, tot = prefix(sel)
    n_past_sel = jnp.sum(tot, axis=0, keepdims=True)
    ci_row = _col_to_row(jnp.broadcast_to(off + tot, (NP, NP)), NP)
    off_row = _col_to_row(jnp.broadcast_to(off, (NP, NP)), NP)
    jcol = lax.broadcasted_iota(I32, (K, 1), 0).astype(F32)
    page_of = jnp.sum(jnp.where(ci_row <= jcol, 1.0, 0.0), axis=1, keepdims=True)
    lane_p = lax.broadcasted_iota(I32, (K, NP), 1).astype(F32)
    onehot = jnp.where(lane_p == page_of, 1.0, 0.0)
    off_j = jnp.sum(onehot * off_row, axis=1, keepdims=True)
    lr = jcol - off_j
    cs_row = jnp.dot(onehot.astype(BF16), cs.astype(BF16), preferred_element_type=F32)
    sel_row = jnp.dot(onehot.astype(BF16), sel.astype(BF16), preferred_element_type=F32)
    lane_o = lax.broadcasted_iota(I32, (K, PG), 1).astype(F32)
    hit = jnp.logical_and(cs_row - 1.0 == lr, sel_row > 0.5)
    off_of = jnp.sum(jnp.where(hit, lane_o, 0.0), axis=1, keepdims=True)
    valid_j = jcol < n_past_sel
    page_of = jnp.where(valid_j, page_of, 0.0)
    off_of = jnp.where(valid_j, off_of, 0.0)
    idxv_ref[...] = jnp.zeros_like(idxv_ref)
    nh = K // LANES
    for hh in range(nh):
        pg_r = _col_to_row(jnp.broadcast_to(page_of[hh * LANES:(hh + 1) * LANES], (LANES, LANES)), LANES)
        of_r = _col_to_row(jnp.broadcast_to(off_of[hh * LANES:(hh + 1) * LANES], (LANES, LANES)), LANES)
        idxv_ref[hh:hh + 1, :] = pg_r.astype(I32)
        idxv_ref[nh + hh:nh + hh + 1, :] = of_r.astype(I32)
    xcp = pltpu.make_async_copy(idxv_ref, idxs_ref, sem_x)
    xcp.start()
    xcp.wait()

    def kcopy(j):
        pg = idxs_ref[j // LANES, j % LANES]
        of = idxs_ref[nh + j // LANES, j % LANES]
        phys = pt_ref[b, pg]
        return (pltpu.make_async_copy(ck_hbm.at[layer, phys, of], kg_ref.at[j], sem_k),
                pltpu.make_async_copy(cv_hbm.at[layer, phys, of], vg_ref.at[j], sem_v))

    def start_j(j, c):
        ck, cv = kcopy(j)
        ck.start()
        cv.start()
        return c

    def wait_j(j, c):
        ck, cv = kcopy(j)
        ck.wait()
        cv.wait()
        return c

    lax.fori_loop(0, K, start_j, 0)
    lax.fori_loop(0, K, wait_j, 0)

    is_new = jnp.logical_and(jnp.logical_not(valid_j), sel_new)
    G = ATT_HEADS // KV_HEADS
    for n in range(KV_HEADS):
        kn = jnp.where(is_new, kn_ref[0, n:n + 1, :], kg_ref[:, n, :])
        vn = jnp.where(is_new, vn_ref[0, n:n + 1, :], vg_ref[:, n, :])
        qn = qa_ref[0, n * G:(n + 1) * G, :]
        s = lax.dot_general(qn, kn.astype(BF16), (((1,), (1,)), ((), ())),
                            preferred_element_type=F32) * (HEAD_DIM ** -0.5)
        m = jnp.max(s, axis=1, keepdims=True)
        p = jnp.exp(s - m)
        l = jnp.sum(p, axis=1, keepdims=True)
        o = jnp.dot(p.astype(BF16), vn.astype(BF16), preferred_element_type=F32) / l
        o_ref[0, n * G:(n + 1) * G, :] = o


def dsa_sample(page_table, qi_s, w_s, kin_s, qa_s, kn_s, vn_s, cache_kidx, cache_k, cache_v, layer):
    DB, n_pages = page_table.shape
    page = cache_k.shape[2]
    L = n_pages * page + 1
    topk = min(TOPK_MAX, L // 4)
    assert topk % LANES == 0 and topk <= n_pages * page
    blk = lambda shp: pl.BlockSpec((1,) + shp, lambda b, pt: (b,) + (0,) * len(shp))
    any_spec = pl.BlockSpec(memory_space=pl.ANY)
    gs = pltpu.PrefetchScalarGridSpec(
        num_scalar_prefetch=1,
        grid=(DB,),
        in_specs=[blk((IDX_HEADS, IDX_DIM)), blk((IDX_HEADS, 1)), blk((1, IDX_DIM)),
                  blk((ATT_HEADS, HEAD_DIM)), blk((KV_HEADS, HEAD_DIM)), blk((KV_HEADS, HEAD_DIM)),
                  any_spec, any_spec, any_spec],
        out_specs=blk((ATT_HEADS, HEAD_DIM)),
        scratch_shapes=[pltpu.VMEM((n_pages, page, IDX_DIM), F32),
                        pltpu.VMEM((n_pages, page), F32),
                        pltpu.VMEM((SUBLANES, LANES), I32),
                        pltpu.SMEM((SUBLANES, LANES), I32),
                        pltpu.VMEM((topk, KV_HEADS, HEAD_DIM), F32),
                        pltpu.VMEM((topk, KV_HEADS, HEAD_DIM), F32),
                        pltpu.SemaphoreType.DMA, pltpu.SemaphoreType.DMA,
                        pltpu.SemaphoreType.DMA, pltpu.SemaphoreType.DMA],
    )
    return pl.pallas_call(
        functools.partial(_dsa_sample_kernel, layer=layer, n_pages=n_pages, page=page, topk=topk),
        grid_spec=gs,
        out_shape=jax.ShapeDtypeStruct((DB, ATT_HEADS, HEAD_DIM), F32),
        compiler_params=_cparams(("arbitrary",)),
        name="dsa_sample",
    )(page_table, qi_s, w_s, kin_s, qa_s, kn_s, vn_s, cache_kidx, cache_k, cache_v)


def _mix_kernel(og_ref, at_ref, gg_ref, gd_ref, x_ref, wbg_ref, wbd_ref, wo_ref, g_ref, b_ref,
                wrh_ref, wrl_ref, br_ref, x1_ref, ri_ref, rw_ref, *, alpha):
    bg = jnp.dot(og_ref[...], wbg_ref[...], preferred_element_type=F32)
    bd = jnp.dot(at_ref[...], wbd_ref[...], preferred_element_type=F32)
    merged = jax.nn.sigmoid(gg_ref[...]) * bg + jax.nn.sigmoid(gd_ref[...]) * bd
    mix = jnp.dot(merged.astype(BF16), wo_ref[...], preferred_element_type=F32)
    x1 = _ln(alpha * x_ref[...] + mix, g_ref[...], b_ref[...])
    x1_ref[...] = x1

    xh = x1.astype(BF16)
    xl = (x1 - xh.astype(F32)).astype(BF16)
    lg = (jnp.dot(xh, wrh_ref[...], preferred_element_type=F32)
          + jnp.dot(xh, wrl_ref[...], preferred_element_type=F32)
          + jnp.dot(xl, wrh_ref[...], preferred_element_type=F32)) + br_ref[...]
    lane = lax.broadcasted_iota(I32, lg.shape, 1)
    big = jnp.int32(1 << 20)
    neg = -jnp.inf
    l_g = jnp.where(lane < N_GROUPS, lg, neg)
    m_g = jnp.max(l_g, axis=1, keepdims=True)
    g_top = jnp.min(jnp.where(l_g == m_g, lane, big), axis=1, keepdims=True)
    pg_top = 1.0 / jnp.sum(jnp.exp(l_g - m_g), axis=1, keepdims=True)
    e_lane = lane - N_GROUPS
    in_grp = jnp.logical_and(e_lane >= g_top * EXP_PER_GROUP, e_lane < (g_top + 1) * EXP_PER_GROUP)
    l_e = jnp.where(in_grp, lg, neg)
    m1 = jnp.max(l_e, axis=1, keepdims=True)
    i1 = jnp.min(jnp.where(l_e == m1, e_lane, big), axis=1, keepdims=True)
    l_e2 = jnp.where(e_lane == i1, neg, l_e)
    m2 = jnp.max(l_e2, axis=1, keepdims=True)
    i2 = jnp.min(jnp.where(l_e2 == m2, e_lane, big), axis=1, keepdims=True)
    e2 = jnp.exp(m2 - m1)
    w1 = pg_top / (1.0 + e2)
    w2 = pg_top * e2 / (1.0 + e2)
    ri_ref[...] = jnp.where(lane == 0, i1, jnp.where(lane == 1, i2, 0))
    rw_ref[...] = jnp.where(lane == 0, w1, jnp.where(lane == 1, w2, 0.0))


def mix_stage(og, att, h, x, wbg, wbd, wo, g, b, wrh, wrl, br, layer, alpha):
    T, D = x.shape
    gw = GLA_HEADS * GLA_DV
    aw = ATT_HEADS * HEAD_DIM
    row = lambda w, c: pl.BlockSpec((TM, w), lambda i: (i, c))
    wsp = lambda r, c: pl.BlockSpec((None, r, c), lambda i: (layer, 0, 0))
    return pl.pallas_call(
        functools.partial(_mix_kernel, alpha=alpha),
        grid=(T // TM,),
        in_specs=[row(gw, 0), row(aw, 0), row(D, C_GG // D), row(D, C_GD // D), row(D, 0),
                  wsp(gw, D), wsp(aw, D), wsp(D, D), wsp(1, D), wsp(1, D),
                  wsp(D, LANES), wsp(D, LANES), wsp(1, LANES)],
        out_specs=[row(D, 0), row(LANES, 0), row(LANES, 0)],
        out_shape=[jax.ShapeDtypeStruct((T, D), F32), jax.ShapeDtypeStruct((T, LANES), I32),
                   jax.ShapeDtypeStruct((T, LANES), F32)],
        compiler_params=_cparams(("parallel",)),
        name="mix_stage",
    )(og, att, h, h, x, wbg, wbd, wo, g, b, wrh, wrl, br)


def _moe_kernel(te_ref, nv_ref, tok_ref, pair_ref, w_ref, x_hbm, wg_ref, wu_ref, wd_ref, y_hbm,
                xbuf, ybuf, gsem, ssem):
    i = pl.program_id(0)
    nv = nv_ref[i]

    @pl.when(i == 0)
    def _():
        xbuf[...] = jnp.zeros_like(xbuf)

    @pl.when(nv > 0)
    def _():
        def gcopy(r):
            return pltpu.make_async_copy(x_hbm.at[pl.ds(tok_ref[0, 0, r], 1)], xbuf.at[pl.ds(r, 1)], gsem)

        def scopy(r):
            return pltpu.make_async_copy(ybuf.at[pl.ds(r, 1)], y_hbm.at[pl.ds(pair_ref[0, 0, r], 1)], ssem)

        lax.fori_loop(0, nv, lambda r, c: (gcopy(r).start(), c)[1], 0)
        lax.fori_loop(0, nv, lambda r, c: (gcopy(r).wait(), c)[1], 0)
        xb = xbuf[...].astype(BF16)
        hg = jnp.dot(xb, wg_ref[...], preferred_element_type=F32)
        hu = jnp.dot(xb, wu_ref[...], preferred_element_type=F32)
        hid = (hg * jax.nn.sigmoid(hg)) * hu * w_ref[0]
        ybuf[...] = jnp.dot(hid.astype(BF16), wd_ref[...], preferred_element_type=F32)
        lax.fori_loop(0, nv, lambda r, c: (scopy(r).start(), c)[1], 0)
        lax.fori_loop(0, nv, lambda r, c: (scopy(r).wait(), c)[1], 0)


def moe_stage(x1, tile_e, tile_nv, slot_tok, slot_pair, slot_w, wg, wu, wd, layer):
    T, D = x1.shape
    n_tiles = tile_e.shape[0]
    F = wg.shape[-1]
    smem_blk = pl.BlockSpec((1, 1, TM), lambda i, te, nv: (i, 0, 0), memory_space=pltpu.SMEM)
    gs = pltpu.PrefetchScalarGridSpec(
        num_scalar_prefetch=2,
        grid=(n_tiles,),
        in_specs=[smem_blk, smem_blk,
                  pl.BlockSpec((1, TM, 1), lambda i, te, nv: (i, 0, 0)),
                  pl.BlockSpec(memory_space=pl.ANY),
                  pl.BlockSpec((None, None, D, F), lambda i, te, nv: (layer, te[i], 0, 0)),
                  pl.BlockSpec((None, None, D, F), lambda i, te, nv: (layer, te[i], 0, 0)),
                  pl.BlockSpec((None, None, F, D), lambda i, te, nv: (layer, te[i], 0, 0))],
        out_specs=pl.BlockSpec(memory_space=pl.ANY),
        scratch_shapes=[pltpu.VMEM((TM, D), F32), pltpu.VMEM((TM, D), F32),
                        pltpu.SemaphoreType.DMA, pltpu.SemaphoreType.DMA],
    )
    return pl.pallas_call(
        _moe_kernel,
        grid_spec=gs,
        out_shape=jax.ShapeDtypeStruct((TOPK_INNER * T, D), F32),
        compiler_params=_cparams(("arbitrary",)),
        name="moe_stage",
    )(tile_e, tile_nv, slot_tok, slot_pair, slot_w, x1, wg, wu, wd)


def moe_plan(route_i, route_w, n_tiles):
    T = route_i.shape[0]
    P = TOPK_INNER * T
    eid = route_i[:, :TOPK_INNER].reshape(P)
    wts = route_w[:, :TOPK_INNER].reshape(P)
    order = jnp.argsort(eid, stable=True).astype(I32)
    counts = jnp.sum(jax.nn.one_hot(eid, N_EXPERTS, dtype=I32), axis=0)
    tiles_per = (counts + TM - 1) // TM
    tile_end = jnp.cumsum(tiles_per)
    tile_start = tile_end - tiles_per
    gstart = jnp.cumsum(counts) - counts
    n_used = tile_end[-1]
    ti = jnp.arange(n_tiles, dtype=I32)
    e_raw = jnp.searchsorted(tile_end, ti, side="right").astype(I32)
    e_last = jnp.searchsorted(tile_end, n_used - 1, side="right").astype(I32)
    used = ti < n_used
    tile_e = jnp.where(used, e_raw, e_last)
    local = ti - tile_start[tile_e]
    tile_nv = jnp.where(used, jnp.clip(counts[tile_e] - local * TM, 0, TM), 0).astype(I32)
    pos = gstart[tile_e][:, None] + local[:, None] * TM + jnp.arange(TM, dtype=I32)[None, :]
    slot_pair = order[jnp.clip(pos, 0, P - 1)]
    slot_tok = slot_pair // TOPK_INNER
    slot_w = wts[slot_pair]
    return (tile_e, tile_nv, slot_tok.reshape(n_tiles, 1, TM), slot_pair.reshape(n_tiles, 1, TM),
            slot_w.reshape(n_tiles, TM, 1))


def _ple_kernel(x1_ref, y_ref, p_ref, g1_ref, b1_ref, wp_ref, wpg_ref, g2_ref, b2_ref,
                x3_ref, x3b_ref, *, alpha):
    D = x1_ref.shape[1]
    ffn = y_ref[:, :D] + y_ref[:, D:]
    x2 = _ln(alpha * x1_ref[...] + ffn, g1_ref[...], b1_ref[...])
    ple = (jnp.dot(p_ref[...], wp_ref[...], preferred_element_type=F32)
           * jax.nn.sigmoid(jnp.dot(x2.astype(BF16), wpg_ref[...], preferred_element_type=F32)))
    x3 = _ln(alpha * x2 + ple, g2_ref[...], b2_ref[...])
    x3_ref[...] = x3
    x3b_ref[...] = x3.astype(BF16)


def ple_stage(x1, y2, pb, g1, b1, wp, wpg, g2, b2, layer, alpha):
    T, D = x1.shape
    PD = pb.shape[1]
    row = lambda w: pl.BlockSpec((TM, w), lambda i: (i, 0))
    wsp = lambda r, c: pl.BlockSpec((None, r, c), lambda i: (layer, 0, 0))
    return pl.pallas_call(
        functools.partial(_ple_kernel, alpha=alpha),
        grid=(T // TM,),
        in_specs=[row(D), row(TOPK_INNER * D), row(PD), wsp(1, D), wsp(1, D), wsp(PD, D), wsp(D, D),
                  wsp(1, D), wsp(1, D)],
        out_specs=[row(D), row(D)],
        out_shape=[jax.ShapeDtypeStruct((T, D), F32), jax.ShapeDtypeStruct((T, D), BF16)],
        compiler_params=_cparams(("parallel",)),
        name="ple_stage",
    )(x1, y2, pb, g1, b1, wp, wpg, g2, b2)


def _rope_tables(pos, dh, period):
    rot = dh // 4
    half = rot // 2
    inv = ROPE_THETA ** (-jnp.arange(half, dtype=F32) / half)
    ang = pos.astype(F32)[:, None] * inv[None, :]
    cos, sin = jnp.cos(ang), jnp.sin(ang)
    T = pos.shape[0]
    z = lambda n: jnp.zeros((T, n), F32)
    c = jnp.concatenate([cos, cos, jnp.ones((T, period - rot), F32)], axis=1)
    s1 = jnp.concatenate([-sin, z(period - half)], axis=1)
    s2 = jnp.concatenate([z(half), sin, z(period - rot)], axis=1)
    rep = LANES // period
    return jnp.stack([jnp.tile(c, (1, rep)), jnp.tile(s1, (1, rep)), jnp.tile(s2, (1, rep))])


def kernel(x_prompt, x_sample, cache_k, cache_v, cache_kidx, state_gla, page_table, p_prompt, p_sample, ln_in_g, ln_in_b, w_in, w_decay, b_decay, gla_norm_g, w_branch_gla, w_branch_dsa, w_out, ln_mix_g, ln_mix_b, w_group, b_group, w_router, b_router, w_gate, w_up, w_down, ln_moe_g, ln_moe_b, w_ple, w_ple_gate, ln_ple_g, ln_ple_b):
    B, S, D = x_prompt.shape
    DB, DS, _ = x_sample.shape
    depth = w_in.shape[0]
    n_past = page_table.shape[1] * cache_k.shape[2]
    assert DS == 1 and S % GLA_CHUNK == 0 and S % TQ == 0 and (B * S) % DB == 0 and DB == SUBLANES
    alpha = (2.0 * depth) ** 0.25
    Tp = B * S
    T = -(-(Tp + DB) // TM_IN) * TM_IN
    pad = T - Tp - DB
    n_tiles = TOPK_INNER * T // TM + N_EXPERTS

    w_main = jnp.concatenate([w_in[..., 0:3072], w_in[..., 3088:6160], w_in[..., 6240:10336]], axis=-1).astype(BF16)
    w_small = jnp.concatenate([w_in[..., 6160:6224], w_in[..., 3072:3088], w_in[..., 6224:6240],
                               jnp.zeros((depth, D, LANES - 96), w_in.dtype)], axis=-1).astype(BF16)
    kw = GLA_HEADS * GLA_DK
    wdp = jnp.zeros((depth, LANES, kw), F32).at[:, L_GA:L_GA + GLA_RANK, :].set(w_decay).astype(BF16)
    bd = b_decay.reshape(depth, 1, kw)
    gn = gla_norm_g.reshape(depth, 1, GLA_DV)
    wbg, wbd, wo = w_branch_gla.astype(BF16), w_branch_dsa.astype(BF16), w_out.astype(BF16)
    w_rt = jnp.concatenate([w_group, w_router, jnp.zeros((depth, D, LANES - N_GROUPS - N_EXPERTS), F32)], axis=-1)
    wrh = w_rt.astype(BF16)
    wrl = (w_rt - wrh.astype(F32)).astype(BF16)
    b_rt = jnp.concatenate([b_group, b_router, jnp.zeros((depth, LANES - N_GROUPS - N_EXPERTS), F32)],
                           axis=-1).reshape(depth, 1, LANES)
    wg, wu, wd = w_gate.astype(BF16), w_up.astype(BF16), w_down.astype(BF16)
    wp, wpg = w_ple.astype(BF16), w_ple_gate.astype(BF16)
    r3 = lambda a: a.reshape(depth, 1, D)

    x_all = jnp.concatenate([x_prompt.reshape(Tp, D), x_sample.reshape(DB, D), jnp.zeros((pad, D), F32)], axis=0)
    pos = jnp.concatenate([jnp.tile(jnp.arange(S, dtype=I32), B), jnp.full((DB,), n_past, I32),
                           jnp.zeros((pad,), I32)])
    t128 = _rope_tables(pos, HEAD_DIM, HEAD_DIM)
    t64 = _rope_tables(pos, IDX_DIM, IDX_DIM)

    x, xb = ln_in(x_all, ln_in_g, ln_in_b)
    ks, vs, kis, ss_p, ss_s = [], [], [], [], []
    for l in range(depth):
        h = in_proj(xb, w_main, l, TN_IN)
        hs = in_proj(xb, w_small, l, LANES)
        ka, va, ki, qab, kab, vab, qib, kie, kio = rope_stage(h, hs, t128, t64)
        og_p, s_p = gla_prompt(h, hs, wdp[l], bd[l], gn[l], B, S)
        og_s, s_s = gla_sample(h, hs, wdp[l], bd[l], gn[l], state_gla, l, Tp, DB)
        at_p = dsa_prompt(qab, qib, hs, kab, vab, kie, kio, B, S)
        sl = slice(Tp, Tp + DB)
        at_s = dsa_sample(page_table, qib[sl].reshape(DB, IDX_HEADS, IDX_DIM),
                          (hs[sl, L_IW:L_IW + IDX_HEADS] * (IDX_HEADS ** -0.5)).reshape(DB, IDX_HEADS, 1),
                          kie[sl, :IDX_DIM].reshape(DB, 1, IDX_DIM),
                          qab[sl].reshape(DB, ATT_HEADS, HEAD_DIM),
                          ka[sl].reshape(DB, KV_HEADS, HEAD_DIM), va[sl].reshape(DB, KV_HEADS, HEAD_DIM),
                          cache_kidx, cache_k, cache_v, l)
        zpad = lambda w: jnp.zeros((pad, w), BF16)
        og = jnp.concatenate([og_p, og_s.astype(BF16), zpad(og_p.shape[1])], axis=0)
        att = jnp.concatenate([at_p, at_s.reshape(DB, -1).astype(BF16), zpad(at_p.shape[1])], axis=0)
        x1, route_i, route_w = mix_stage(og, att, h, x, wbg, wbd, wo, r3(ln_mix_g), r3(ln_mix_b),
                                         wrh, wrl, b_rt, l, alpha)
        plan = moe_plan(route_i, route_w, n_tiles)
        y = moe_stage(x1, *plan, wg, wu, wd, l)
        p_all = jnp.concatenate([p_prompt[l].reshape(Tp, -1), p_sample[l].reshape(DB, -1),
                                 jnp.zeros((pad, p_prompt.shape[-1]), F32)], axis=0).astype(BF16)
        x, xb = ple_stage(x1, y.reshape(T, TOPK_INNER * D), p_all, r3(ln_moe_g), r3(ln_moe_b), wp, wpg,
                          r3(ln_ple_g), r3(ln_ple_b), l, alpha)
        ks.append(ka)
        vs.append(va)
        kis.append(ki)
        ss_p.append(s_p)
        ss_s.append(s_s)

    kst, vst, kist = jnp.stack(ks), jnp.stack(vs), jnp.stack(kis)
    y_prompt = x[:Tp].reshape(B, S, D)
    y_sample = x[Tp:Tp + DB].reshape(DB, DS, D)
    k_prompt = kst[:, :Tp].reshape(depth, B, S, KV_HEADS, HEAD_DIM)
    v_prompt = vst[:, :Tp].reshape(depth, B, S, KV_HEADS, HEAD_DIM)
    kidx_prompt = kist[:, :Tp].reshape(depth, B, S, IDX_DIM)
    k_sample = kst[:, Tp:Tp + DB].reshape(depth, DB, DS, KV_HEADS, HEAD_DIM)
    v_sample = vst[:, Tp:Tp + DB].reshape(depth, DB, DS, KV_HEADS, HEAD_DIM)
    kidx_sample = kist[:, Tp:Tp + DB].reshape(depth, DB, DS, IDX_DIM)
    return (y_prompt, y_sample, k_prompt, v_prompt, kidx_prompt, jnp.stack(ss_p),
            k_sample, v_sample, kidx_sample, jnp.stack(ss_s))
```

```python
import functools

import jax
import jax.numpy as jnp
from jax import lax
from jax.experimental import pallas as pl
from jax.experimental.pallas import tpu as pltpu

F32 = jnp.float32
BF16 = jnp.bfloat16
I32 = jnp.int32

GLA_HEADS = 4
GLA_DK = 128
GLA_DV = 256
GLA_RANK = 16
GLA_TAU = 16.0
ATT_HEADS = 8
KV_HEADS = 4
HEAD_DIM = 128
IDX_HEADS = 16
IDX_DIM = 64
TOPK_MAX = 256
ROPE_THETA = 500000.0
N_GROUPS = 4
EXP_PER_GROUP = 8
N_EXPERTS = N_GROUPS * EXP_PER_GROUP
TOPK_INNER = 2
LN_EPS = 1e-5

LANES = 128
SUBLANES = 8
TM_IN = 1024
TN_IN = 1024
TM = 256
GLA_CHUNK = 128
TQ = 128
KEY_BUCKET = 512
SCORE_PAGES = 8
RADIX_BITS = 4
DMA_UNROLL = 8
TK_S = 512
VMEM_LIMIT = 56 * 1024 * 1024
INT_MIN = -(2 ** 31)

C_GQ, C_GK, C_GV, C_GR = 0, 512, 1024, 2048
C_AQ, C_AK, C_AV, C_IQ = 3072, 4096, 4608, 5120
C_GG, C_GD = 6144, 8192
N_MAIN = 10240
W_IN_TILE_OFFSETS = (0, 1024, 2048, 3088, 4112, 5136, 6240, 7264, 8288, 9312)
L_IK, L_GA, L_IW = 0, 64, 80


def _cparams(sem):
    return pltpu.CompilerParams(dimension_semantics=sem, vmem_limit_bytes=VMEM_LIMIT)


def _bdot(a, b):
    return jnp.dot(a.astype(BF16), b.astype(BF16), preferred_element_type=F32)


def _bdot_nt(a, b):
    return lax.dot_general(a.astype(BF16), b.astype(BF16), (((1,), (1,)), ((), ())),
                           preferred_element_type=F32)


def _hilo(x):
    hi = x.astype(BF16)
    return hi, (x - hi.astype(F32)).astype(BF16)


def _dot3(a, w):
    ah, al = _hilo(a)
    wh, wl = _hilo(w)
    d = lambda p, q: jnp.dot(p, q, preferred_element_type=F32)
    return d(ah, wh) + (d(al, wh) + d(ah, wl))


def _dot3_nt(a, wt):
    ah, al = _hilo(a)
    wh, wl = _hilo(wt)
    d = lambda p, q: lax.dot_general(p, q, (((1,), (1,)), ((), ())), preferred_element_type=F32)
    return d(ah, wh) + (d(al, wh) + d(ah, wl))


def _split3(x):
    hi = x.astype(BF16)
    r = x - hi.astype(F32)
    mid = r.astype(BF16)
    lo = (r - mid.astype(F32)).astype(BF16)
    return hi, mid, lo


def _ln(x, g, b):
    mu = jnp.mean(x, axis=-1, keepdims=True)
    xc = x - mu
    var = jnp.mean(xc * xc, axis=-1, keepdims=True)
    return xc * lax.rsqrt(var + LN_EPS) * g + b


def _col_to_row(col, n):
    eye = lax.broadcasted_iota(I32, (n, n), 0) == lax.broadcasted_iota(I32, (n, n), 1)
    return jnp.sum(jnp.where(eye, col, 0.0), axis=0, keepdims=True)


def _row_to_col(row, n):
    eye = lax.broadcasted_iota(I32, (n, n), 0) == lax.broadcasted_iota(I32, (n, n), 1)
    return jnp.sum(jnp.where(eye, row, 0.0), axis=1, keepdims=True)


def _sort_key(x):
    bits = lax.bitcast_convert_type(x, I32)
    return jnp.where(bits < 0, bits ^ jnp.int32(0x7FFFFFFF), bits)


def _ln_in_kernel(x_ref, g_ref, b_ref, o_ref, ob_ref):
    y = _ln(x_ref[...], g_ref[...], b_ref[...])
    o_ref[...] = y
    ob_ref[...] = y.astype(BF16)


def ln_in(x, g, b, tm):
    T, D = x.shape
    return pl.pallas_call(
        _ln_in_kernel,
        grid=(T // tm,),
        in_specs=[pl.BlockSpec((tm, D), lambda i: (i, 0)),
                  pl.BlockSpec((1, D), lambda i: (0, 0)),
                  pl.BlockSpec((1, D), lambda i: (0, 0))],
        out_specs=[pl.BlockSpec((tm, D), lambda i: (i, 0)),
                   pl.BlockSpec((tm, D), lambda i: (i, 0))],
        out_shape=[jax.ShapeDtypeStruct((T, D), F32), jax.ShapeDtypeStruct((T, D), BF16)],
        compiler_params=_cparams(("parallel",)),
        name="ln_in",
    )(x, g.reshape(1, D), b.reshape(1, D))


def _in_proj_kernel(off_ref, x_ref, wt_ref, o_ref, w_s):
    @pl.when(pl.program_id(1) == 0)
    def _():
        w_s[...] = wt_ref[0].T.astype(BF16)

    o_ref[...] = jnp.dot(x_ref[...], w_s[...], preferred_element_type=F32)


def _s_in_proj_kernel(off_ref, x_ref, wt_ref, o_ref):
    o_ref[...] = _dot3_nt(x_ref[...], wt_ref[0])


def _w_in_tile_spec(layer, D):
    return pl.BlockSpec((pl.Element(1), pl.Element(TN_IN), pl.Element(D)),
                        lambda j, *a: (layer, pl.multiple_of(a[-1][j], SUBLANES), 0))


def in_proj_main(xb, w_t, layer, tm):
    T, D = xb.shape
    offs = jnp.asarray(W_IN_TILE_OFFSETS, I32)
    gs = pltpu.PrefetchScalarGridSpec(
        num_scalar_prefetch=1,
        grid=(len(W_IN_TILE_OFFSETS), T // tm),
        in_specs=[pl.BlockSpec((tm, D), lambda j, i, off: (i, 0)), _w_in_tile_spec(layer, D)],
        out_specs=pl.BlockSpec((tm, TN_IN), lambda j, i, off: (i, j)),
        scratch_shapes=[pltpu.VMEM((D, TN_IN), BF16)],
    )
    return pl.pallas_call(
        _in_proj_kernel,
        grid_spec=gs,
        out_shape=jax.ShapeDtypeStruct((T, N_MAIN), F32),
        compiler_params=_cparams(("parallel", "arbitrary")),
        name="in_proj_main",
    )(offs, xb, w_t)


def s_in_proj_main(x, w_t, layer):
    R, D = x.shape
    offs = jnp.asarray(W_IN_TILE_OFFSETS, I32)
    gs = pltpu.PrefetchScalarGridSpec(
        num_scalar_prefetch=1,
        grid=(len(W_IN_TILE_OFFSETS),),
        in_specs=[pl.BlockSpec((R, D), lambda j, off: (0, 0)), _w_in_tile_spec(layer, D)],
        out_specs=pl.BlockSpec((R, TN_IN), lambda j, off: (0, j)),
    )
    return pl.pallas_call(
        _s_in_proj_kernel,
        grid_spec=gs,
        out_shape=jax.ShapeDtypeStruct((R, N_MAIN), F32),
        compiler_params=_cparams(("parallel",)),
        name="s_in_proj_main",
    )(offs, x, w_t)


def _mm_kernel(x_ref, w_ref, o_ref):
    o_ref[...] = jnp.dot(x_ref[...], w_ref[...], preferred_element_type=F32)


def _s_mm_kernel(x_ref, w_ref, o_ref):
    o_ref[...] = _dot3(x_ref[...], w_ref[...])


def in_proj_small(x, w, layer, tm, precise):
    T, D = x.shape
    N = w.shape[-1]
    return pl.pallas_call(
        _s_mm_kernel if precise else _mm_kernel,
        grid=(T // tm,),
        in_specs=[pl.BlockSpec((tm, D), lambda i: (i, 0)),
                  pl.BlockSpec((None, D, N), lambda i: (layer, 0, 0))],
        out_specs=pl.BlockSpec((tm, N), lambda i: (i, 0)),
        out_shape=jax.ShapeDtypeStruct((T, N), F32),
        compiler_params=_cparams(("parallel",)),
        name="in_proj_small",
    )(x, w)


def _rope_tile(x, c, s1, s2, shift):
    w = x.shape[-1]
    xm = pltpu.roll(x, w - shift, axis=1)
    xp = pltpu.roll(x, shift, axis=1)
    return x * c + xm * s1 + xp * s2


def _rope_kernel(aq_ref, ak_ref, av_ref, iq_ref, hs_ref, t128_ref, t64_ref,
                 ka_ref, va_ref, ki_ref, qa_ref, kab_ref, vab_ref, qib_ref, kie_ref, kio_ref):
    c1, s1a, s1b = t128_ref[0], t128_ref[1], t128_ref[2]
    c2, s2a, s2b = t64_ref[0], t64_ref[1], t64_ref[2]
    for hd in range(ATT_HEADS):
        sl = slice(hd * HEAD_DIM, (hd + 1) * HEAD_DIM)
        qa_ref[:, sl] = _rope_tile(aq_ref[:, sl], c1, s1a, s1b, HEAD_DIM // 8).astype(qa_ref.dtype)
    for hd in range(KV_HEADS):
        sl = slice(hd * HEAD_DIM, (hd + 1) * HEAD_DIM)
        kr = _rope_tile(ak_ref[:, sl], c1, s1a, s1b, HEAD_DIM // 8)
        ka_ref[:, sl] = kr
        kab_ref[:, sl] = kr.astype(BF16)
    v = av_ref[...]
    va_ref[...] = v
    vab_ref[...] = v.astype(BF16)
    for pr in range(IDX_HEADS * IDX_DIM // LANES):
        sl = slice(pr * LANES, (pr + 1) * LANES)
        qr = _rope_tile(iq_ref[:, sl], c2, s2a, s2b, IDX_DIM // 8) * (IDX_DIM ** -0.5)
        qib_ref[:, sl] = qr.astype(qib_ref.dtype)
    kir = _rope_tile(hs_ref[...], c2, s2a, s2b, IDX_DIM // 8)
    ki_ref[...] = kir[:, :IDX_DIM]
    lane = lax.broadcasted_iota(I32, kir.shape, 1)
    ke = jnp.where(lane < IDX_DIM, kir, 0.0)
    kie_ref[...] = ke.astype(BF16)
    kio_ref[...] = pltpu.roll(ke, IDX_DIM, axis=1).astype(BF16)


def rope_stage(h, hs, t128, t64, tm, q_dtype):
    T = h.shape[0]
    aw, kw, iw = ATT_HEADS * HEAD_DIM, KV_HEADS * HEAD_DIM, IDX_HEADS * IDX_DIM
    row = lambda w, c: pl.BlockSpec((tm, w), lambda i: (i, c))
    tab = pl.BlockSpec((3, tm, LANES), lambda i: (0, i, 0))
    return pl.pallas_call(
        _rope_kernel,
        grid=(T // tm,),
        in_specs=[row(aw, C_AQ // aw), row(kw, C_AK // kw), row(kw, C_AV // kw), row(iw, C_IQ // iw),
                  row(LANES, 0), tab, tab],
        out_specs=[row(kw, 0), row(kw, 0), row(IDX_DIM, 0), row(aw, 0), row(kw, 0), row(kw, 0),
                   row(iw, 0), row(LANES, 0), row(LANES, 0)],
        out_shape=[jax.ShapeDtypeStruct((T, kw), F32), jax.ShapeDtypeStruct((T, kw), F32),
                   jax.ShapeDtypeStruct((T, IDX_DIM), F32), jax.ShapeDtypeStruct((T, aw), q_dtype),
                   jax.ShapeDtypeStruct((T, kw), BF16), jax.ShapeDtypeStruct((T, kw), BF16),
                   jax.ShapeDtypeStruct((T, iw), q_dtype), jax.ShapeDtypeStruct((T, LANES), BF16),
                   jax.ShapeDtypeStruct((T, LANES), BF16)],
        compiler_params=_cparams(("parallel",)),
        name="rope_stage",
    )(h, h, h, h, hs, t128, t64)


def _log_decay(hs, wdp_ref, bd_ref, precise=False):
    z = (_dot3 if precise else _bdot)(hs, wdp_ref[...]) + bd_ref[...]
    return (jnp.minimum(z, 0.0) - jnp.log1p(jnp.exp(-jnp.abs(z)))) * (1.0 / GLA_TAU)


def _gla_out(o, gn, gr):
    on = o * lax.rsqrt(jnp.mean(o * o, axis=-1, keepdims=True) + LN_EPS) * gn
    return on * (gr * jax.nn.sigmoid(gr))


def _gla_prompt_kernel(hq_ref, hk_ref, hv_ref, hr_ref, hs_ref, wdp_ref, bd_ref, gn_ref,
                       og_ref, sfin_ref, s_ref):
    c = pl.program_id(1)
    C = GLA_CHUNK

    @pl.when(c == 0)
    def _():
        s_ref[...] = jnp.zeros_like(s_ref)

    la_all = _log_decay(hs_ref[...], wdp_ref, bd_ref)
    r_i = lax.broadcasted_iota(I32, (C, C), 0)
    c_i = lax.broadcasted_iota(I32, (C, C), 1)
    causal = r_i >= c_i
    tri = jnp.where(causal, 1.0, 0.0).astype(BF16)
    gn = gn_ref[...]
    for hd in range(GLA_HEADS):
        ks = slice(hd * GLA_DK, (hd + 1) * GLA_DK)
        vs = slice(hd * GLA_DV, (hd + 1) * GLA_DV)
        la = la_all[:, ks]
        hi, mid, lo = _split3(la)
        b = (jnp.dot(tri, hi, preferred_element_type=F32)
             + jnp.dot(tri, mid, preferred_element_type=F32)
             + jnp.dot(tri, lo, preferred_element_type=F32))
        eb = jnp.exp(b)
        qe = hq_ref[:, ks] * (GLA_DK ** -0.5) * eb
        ke = hk_ref[:, ks] * jnp.exp(-b)
        v = hv_ref[:, vs]
        s_old = s_ref[hd]
        att = jnp.where(causal, _bdot_nt(qe, ke), 0.0)
        o = _bdot(att, v) + _bdot(qe, s_old)
        og_ref[:, vs] = _gla_out(o, gn, hr_ref[:, vs]).astype(BF16)
        eb_last = eb[C - 1:C, :]
        kdec = ke * eb_last
        upd = lax.dot_general(kdec.astype(BF16), v.astype(BF16), (((0,), (0,)), ((), ())),
                              preferred_element_type=F32)
        s_new = s_old * _row_to_col(jnp.broadcast_to(eb_last, (GLA_DK, GLA_DK)), GLA_DK) + upd
        s_ref[hd] = s_new

        @pl.when(c == pl.num_programs(1) - 1)
        def _():
            sfin_ref[0, hd] = s_new


def gla_prompt(h, hs, wdp, bd, gn, B, S):
    nC = S // GLA_CHUNK
    kw, vw = GLA_HEADS * GLA_DK, GLA_HEADS * GLA_DV
    row = lambda w, col: pl.BlockSpec((GLA_CHUNK, w), lambda b, c: (b * nC + c, col))
    cst = lambda shp: pl.BlockSpec(shp, lambda b, c: (0,) * len(shp))
    return pl.pallas_call(
        _gla_prompt_kernel,
        grid=(B, nC),
        in_specs=[row(kw, C_GQ // kw), row(kw, C_GK // kw), row(vw, C_GV // vw), row(vw, C_GR // vw),
                  row(LANES, 0), cst((LANES, kw)), cst((1, kw)), cst((1, GLA_DV))],
        out_specs=[row(vw, 0),
                   pl.BlockSpec((1, GLA_HEADS, GLA_DK, GLA_DV), lambda b, c: (b, 0, 0, 0))],
        out_shape=[jax.ShapeDtypeStruct((B * S, vw), BF16),
                   jax.ShapeDtypeStruct((B, GLA_HEADS, GLA_DK, GLA_DV), F32)],
        scratch_shapes=[pltpu.VMEM((GLA_HEADS, GLA_DK, GLA_DV), F32)],
        compiler_params=_cparams(("arbitrary", "arbitrary")),
        name="gla_prompt",
    )(h, h, h, h, hs, wdp, bd, gn)


def _gla_sample_kernel(hq_ref, hk_ref, hv_ref, hr_ref, hs_ref, wdp_ref, bd_ref, gn_ref, s0_ref,
                       og_ref, s1_ref):
    b = pl.program_id(0)
    la_all = _log_decay(hs_ref[...], wdp_ref, bd_ref, precise=True)
    gn = gn_ref[...]
    rowsel = lax.broadcasted_iota(I32, (hq_ref.shape[0], 1), 0) == b

    def pick(x):
        return jnp.sum(jnp.where(rowsel, x, 0.0), axis=0, keepdims=True)

    for hd in range(GLA_HEADS):
        ks = slice(hd * GLA_DK, (hd + 1) * GLA_DK)
        vs = slice(hd * GLA_DV, (hd + 1) * GLA_DV)
        a_row = jnp.exp(pick(la_all[:, ks]))
        q_row = pick(hq_ref[:, ks]) * (GLA_DK ** -0.5)
        k_row = pick(hk_ref[:, ks])
        v_row = pick(hv_ref[:, vs])
        bc = lambda r: _row_to_col(jnp.broadcast_to(r, (GLA_DK, GLA_DK)), GLA_DK)
        s_new = s0_ref[0, hd] * bc(a_row) + bc(k_row) * v_row
        s1_ref[0, hd] = s_new
        o = jnp.sum(bc(q_row) * s_new, axis=0, keepdims=True)
        og_ref[pl.ds(b, 1), vs] = _gla_out(o, gn, pick(hr_ref[:, vs]))


def gla_sample(h, hs, wdp, bd, gn, state, layer):
    DB = h.shape[0]
    kw, vw = GLA_HEADS * GLA_DK, GLA_HEADS * GLA_DV
    row = lambda w, col: pl.BlockSpec((DB, w), lambda b: (0, col))
    cst = lambda shp: pl.BlockSpec(shp, lambda b: (0,) * len(shp))
    return pl.pallas_call(
        _gla_sample_kernel,
        grid=(DB,),
        in_specs=[row(kw, C_GQ // kw), row(kw, C_GK // kw), row(vw, C_GV // vw), row(vw, C_GR // vw),
                  row(LANES, 0), cst((LANES, kw)), cst((1, kw)), cst((1, GLA_DV)),
                  pl.BlockSpec((None, 1, GLA_HEADS, GLA_DK, GLA_DV), lambda b: (layer, b, 0, 0, 0))],
        out_specs=[pl.BlockSpec((DB, vw), lambda b: (0, 0)),
                   pl.BlockSpec((1, GLA_HEADS, GLA_DK, GLA_DV), lambda b: (b, 0, 0, 0))],
        out_shape=[jax.ShapeDtypeStruct((DB, vw), F32),
                   jax.ShapeDtypeStruct((DB, GLA_HEADS, GLA_DK, GLA_DV), F32)],
        compiler_params=_cparams(("arbitrary",)),
        name="gla_sample",
    )(h, h, h, h, hs, wdp, bd, gn, state)


def _kth_threshold(key, topk):
    R = key.shape[0]

    def body(i, t):
        cand = t + jnp.left_shift(jnp.int32(1), 31 - i)
        cnt = jnp.sum(jnp.where(key >= cand, 1.0, 0.0), axis=1, keepdims=True)
        return jnp.where(cnt >= float(topk), cand, t)

    return lax.fori_loop(0, 32, body, jnp.full((R, 1), INT_MIN, I32))


def _dsa_prompt_body(qb, nk, qa_ref, qi_ref, hs_ref, ka_ref, va_ref, kie_ref, kio_ref, o_ref, topk):
    hs = hs_ref[...]
    acc = jnp.zeros((TQ, nk), F32)
    for hd in range(IDX_HEADS):
        pr = hd // 2
        q = qi_ref[:, pr * LANES:(pr + 1) * LANES]
        kk = kie_ref[:nk, :] if hd % 2 == 0 else kio_ref[:nk, :]
        s = lax.dot_general(q, kk, (((1,), (1,)), ((), ())), preferred_element_type=F32)
        w = hs[:, L_IW + hd:L_IW + hd + 1] * (IDX_HEADS ** -0.5)
        acc = acc + w * jnp.maximum(s, 0.0)
    qpos = qb * TQ + lax.broadcasted_iota(I32, (TQ, nk), 0)
    kpos = lax.broadcasted_iota(I32, (TQ, nk), 1)
    vis = kpos <= qpos
    if nk > topk:
        key = jnp.where(vis, _sort_key(acc), INT_MIN)
        thr = _kth_threshold(key, topk)
        sel = jnp.logical_and(key >= thr, vis)
    else:
        sel = vis
    G = ATT_HEADS // KV_HEADS
    for n in range(KV_HEADS):
        ksl = slice(n * HEAD_DIM, (n + 1) * HEAD_DIM)
        kn = ka_ref[:nk, ksl]
        vn = va_ref[:nk, ksl]
        for g in range(G):
            hsl = slice((n * G + g) * HEAD_DIM, (n * G + g + 1) * HEAD_DIM)
            s = lax.dot_general(qa_ref[:, hsl], kn, (((1,), (1,)), ((), ())),
                                preferred_element_type=F32) * (HEAD_DIM ** -0.5)
            s = jnp.where(sel, s, -jnp.inf)
            m = jnp.max(s, axis=1, keepdims=True)
            p = jnp.exp(s - m)
            l = jnp.sum(p, axis=1, keepdims=True)
            o = jnp.dot(p.astype(BF16), vn, preferred_element_type=F32) / l
            o_ref[:, hsl] = o.astype(BF16)


def _dsa_prompt_kernel(qa_ref, qi_ref, hs_ref, ka_ref, va_ref, kie_ref, kio_ref, o_ref, *, topk, bucket):
    qb = pl.program_id(1)
    S = ka_ref.shape[0]
    for bi in range(S // bucket):
        @pl.when(qb // (bucket // TQ) == bi)
        def _():
            _dsa_prompt_body(qb, (bi + 1) * bucket, qa_ref, qi_ref, hs_ref, ka_ref, va_ref, kie_ref, kio_ref,
                             o_ref, topk)


def dsa_prompt(qab, qib, hs, kab, vab, kie, kio, B, S):
    nq = S // TQ
    aw, kw, iw = ATT_HEADS * HEAD_DIM, KV_HEADS * HEAD_DIM, IDX_HEADS * IDX_DIM
    topk = min(TOPK_MAX, S // 4)
    bucket = min(KEY_BUCKET, S)
    assert S % bucket == 0 and bucket % TQ == 0
    qrow = lambda w: pl.BlockSpec((TQ, w), lambda b, q: (b * nq + q, 0))
    seq = lambda w: pl.BlockSpec((S, w), lambda b, q: (b, 0))
    return pl.pallas_call(
        functools.partial(_dsa_prompt_kernel, topk=topk, bucket=bucket),
        grid=(B, nq),
        in_specs=[qrow(aw), qrow(iw), qrow(LANES), seq(kw), seq(kw), seq(LANES), seq(LANES)],
        out_specs=qrow(aw),
        out_shape=jax.ShapeDtypeStruct((B * S, aw), BF16),
        compiler_params=_cparams(("parallel", "parallel")),
        name="dsa_prompt",
    )(qab, qib, hs, kab, vab, kie, kio)


def _dsa_sample_kernel(pt_ref, qi_ref, w_ref, kin_ref, qa_ref, kn_ref, vn_ref,
                       ckidx_hbm, ck_hbm, cv_hbm, o_ref,
                       kbuf, sc_ref, idxv_ref, idxs_ref, kg_ref, vg_ref, sem_i, sem_x, sem_k, sem_v,
                       *, layer, n_pages, page, topk):
    b = pl.program_id(0)
    NP, PG, K = n_pages, page, topk

    def icopy(p):
        return pltpu.make_async_copy(ckidx_hbm.at[layer, pt_ref[b, p]], kbuf.at[p], sem_i)

    lax.fori_loop(0, NP, lambda p, c: (icopy(p).start(), c)[1], 0)
    lax.fori_loop(0, NP, lambda p, c: (icopy(p).wait(), c)[1], 0)

    q = qi_ref[0]
    wcol = w_ref[0]

    def score_chunk(c, carry):
        pages = kbuf[pl.ds(c * SCORE_PAGES, SCORE_PAGES)]
        kc = jnp.concatenate([pages[j] for j in range(SCORE_PAGES)], axis=1)
        s = _dot3(q, kc)
        r = jnp.sum(wcol * jnp.maximum(s, 0.0), axis=0, keepdims=True)
        for j in range(SCORE_PAGES):
            sc_ref[pl.ds(c * SCORE_PAGES + j, 1), :] = r[:, j * PG:(j + 1) * PG]
        return carry

    lax.fori_loop(0, NP // SCORE_PAGES, score_chunk, 0)
    s_new = jnp.sum(q * kin_ref[0], axis=1, keepdims=True)
    s_new = jnp.sum(wcol * jnp.maximum(s_new, 0.0), axis=0, keepdims=True)

    key = _sort_key(sc_ref[...])
    key_new = _sort_key(s_new)

    def body(i, t):
        shift = 32 - RADIX_BITS * (i + 1)
        digit = jnp.zeros((1, 1), I32)
        for d in range(1, 2 ** RADIX_BITS):
            cand = t + jnp.left_shift(jnp.int32(d), shift)
            cnt = jnp.sum(jnp.sum(jnp.where(key >= cand, 1.0, 0.0), axis=0, keepdims=True), axis=1, keepdims=True)
            cnt = cnt + jnp.where(key_new >= cand, 1.0, 0.0)
            digit = digit + jnp.where(cnt >= float(K), 1, 0)
        return t + jnp.left_shift(digit, shift)

    thr = lax.fori_loop(0, 32 // RADIX_BITS, body, jnp.full((1, 1), INT_MIN, I32))
    gt = jnp.where(key > thr, 1.0, 0.0)
    eq = jnp.where(key == thr, 1.0, 0.0)
    n_gt = (jnp.sum(jnp.sum(gt, axis=1, keepdims=True), axis=0, keepdims=True)
            + jnp.where(key_new > thr, 1.0, 0.0))
    need = float(K) - n_gt

    r_p = lax.broadcasted_iota(I32, (PG, PG), 0)
    c_p = lax.broadcasted_iota(I32, (PG, PG), 1)
    ut = jnp.where(r_p <= c_p, 1.0, 0.0).astype(BF16)
    r_n = lax.broadcasted_iota(I32, (NP, NP), 0)
    c_n = lax.broadcasted_iota(I32, (NP, NP), 1)
    slt = jnp.where(c_n < r_n, 1.0, 0.0).astype(BF16)

    def prefix(m):
        cs = jnp.dot(m.astype(BF16), ut, preferred_element_type=F32)
        tot = cs[:, PG - 1:PG]
        off = jnp.dot(slt, jnp.broadcast_to(tot, (NP, PG)).astype(BF16), preferred_element_type=F32)[:, 0:1]
        return cs, off, tot

    cs_e, off_e, tot_e = prefix(eq)
    tie_rank = off_e + cs_e - eq
    sel = jnp.maximum(gt, jnp.where(tie_rank < need, eq, 0.0))
    n_eq_past = jnp.sum(tot_e, axis=0, keepdims=True)
    sel_new = jnp.logical_or(key_new > thr, jnp.logical_and(key_new == thr, n_eq_past < need))

    cs, off, tot = prefix(sel)
    n_past_sel = jnp.sum(tot, axis=0, keepdims=True)
    ci_row = _col_to_row(jnp.broadcast_to(off + tot, (NP, NP)), NP)
    off_row = _col_to_row(jnp.broadcast_to(off, (NP, NP)), NP)
    jcol = lax.broadcasted_iota(I32, (K, 1), 0).astype(F32)
    page_of = jnp.sum(jnp.where(ci_row <= jcol, 1.0, 0.0), axis=1, keepdims=True)
    lane_p = lax.broadcasted_iota(I32, (K, NP), 1).astype(F32)
    onehot = jnp.where(lane_p == page_of, 1.0, 0.0)
    off_j = jnp.sum(onehot * off_row, axis=1, keepdims=True)
    lr = jcol - off_j
    cs_row = jnp.dot(onehot.astype(BF16), cs.astype(BF16), preferred_element_type=F32)
    sel_row = jnp.dot(onehot.astype(BF16), sel.astype(BF16), preferred_element_type=F32)
    lane_o = lax.broadcasted_iota(I32, (K, PG), 1).astype(F32)
    hit = jnp.logical_and(cs_row - 1.0 == lr, sel_row > 0.5)
    off_of = jnp.sum(jnp.where(hit, lane_o, 0.0), axis=1, keepdims=True)
    valid_j = jcol < n_past_sel
    page_of = jnp.where(valid_j, page_of, 0.0)
    off_of = jnp.where(valid_j, off_of, 0.0)
    idxv_ref[...] = jnp.zeros_like(idxv_ref)
    nh = K // LANES
    for hh in range(nh):
        pg_r = _col_to_row(jnp.broadcast_to(page_of[hh * LANES:(hh + 1) * LANES], (LANES, LANES)), LANES)
        of_r = _col_to_row(jnp.broadcast_to(off_of[hh * LANES:(hh + 1) * LANES], (LANES, LANES)), LANES)
        idxv_ref[hh:hh + 1, :] = pg_r.astype(I32)
        idxv_ref[nh + hh:nh + hh + 1, :] = of_r.astype(I32)
    xcp = pltpu.make_async_copy(idxv_ref, idxs_ref, sem_x)
    xcp.start()
    xcp.wait()

    def kcopy(j):
        pg = idxs_ref[j // LANES, j % LANES]
        of = idxs_ref[nh + j // LANES, j % LANES]
        phys = pt_ref[b, pg]
        return (pltpu.make_async_copy(ck_hbm.at[layer, phys, of], kg_ref.at[j], sem_k),
                pltpu.make_async_copy(cv_hbm.at[layer, phys, of], vg_ref.at[j], sem_v))

    def start_j(c, carry):
        for u in range(DMA_UNROLL):
            ck, cv = kcopy(c * DMA_UNROLL + u)
            ck.start()
            cv.start()
        return carry

    def wait_j(c, carry):
        for u in range(DMA_UNROLL):
            ck, cv = kcopy(c * DMA_UNROLL + u)
            ck.wait()
            cv.wait()
        return carry

    lax.fori_loop(0, K // DMA_UNROLL, start_j, 0)
    lax.fori_loop(0, K // DMA_UNROLL, wait_j, 0)

    is_new = jnp.logical_and(jnp.logical_not(valid_j), sel_new)
    G = ATT_HEADS // KV_HEADS
    for n in range(KV_HEADS):
        kn = jnp.where(is_new, kn_ref[0, n:n + 1, :], kg_ref[:, n, :])
        vn = jnp.where(is_new, vn_ref[0, n:n + 1, :], vg_ref[:, n, :])
        qn = qa_ref[0, n * G:(n + 1) * G, :]
        s = _dot3_nt(qn, kn) * (HEAD_DIM ** -0.5)
        m = jnp.max(s, axis=1, keepdims=True)
        p = jnp.exp(s - m)
        l = jnp.sum(p, axis=1, keepdims=True)
        o_ref[0, n * G:(n + 1) * G, :] = _dot3(p, vn) / l


def dsa_sample(page_table, qi_s, w_s, kin_s, qa_s, kn_s, vn_s, cache_kidx_t, cache_k, cache_v, layer):
    DB, n_pages = page_table.shape
    page = cache_k.shape[2]
    L = n_pages * page + 1
    topk = min(TOPK_MAX, L // 4)
    assert topk % LANES == 0 and topk <= n_pages * page and n_pages % SCORE_PAGES == 0
    blk = lambda shp: pl.BlockSpec((1,) + shp, lambda b, pt: (b,) + (0,) * len(shp))
    any_spec = pl.BlockSpec(memory_space=pl.ANY)
    gs = pltpu.PrefetchScalarGridSpec(
        num_scalar_prefetch=1,
        grid=(DB,),
        in_specs=[blk((IDX_HEADS, IDX_DIM)), blk((IDX_HEADS, 1)), blk((1, IDX_DIM)),
                  blk((ATT_HEADS, HEAD_DIM)), blk((KV_HEADS, HEAD_DIM)), blk((KV_HEADS, HEAD_DIM)),
                  any_spec, any_spec, any_spec],
        out_specs=blk((ATT_HEADS, HEAD_DIM)),
        scratch_shapes=[pltpu.VMEM((n_pages, IDX_DIM, page), F32),
                        pltpu.VMEM((n_pages, page), F32),
                        pltpu.VMEM((SUBLANES, LANES), I32),
                        pltpu.SMEM((SUBLANES, LANES), I32),
                        pltpu.VMEM((topk, KV_HEADS, HEAD_DIM), F32),
                        pltpu.VMEM((topk, KV_HEADS, HEAD_DIM), F32),
                        pltpu.SemaphoreType.DMA, pltpu.SemaphoreType.DMA,
                        pltpu.SemaphoreType.DMA, pltpu.SemaphoreType.DMA],
    )
    return pl.pallas_call(
        functools.partial(_dsa_sample_kernel, layer=layer, n_pages=n_pages, page=page, topk=topk),
        grid_spec=gs,
        out_shape=jax.ShapeDtypeStruct((DB, ATT_HEADS, HEAD_DIM), F32),
        compiler_params=_cparams(("arbitrary",)),
        name="dsa_sample",
    )(page_table, qi_s, w_s, kin_s, qa_s, kn_s, vn_s, cache_kidx_t, cache_k, cache_v)


def _route(lg, ri_ref, rw_ref):
    lane = lax.broadcasted_iota(I32, lg.shape, 1)
    big = jnp.int32(1 << 20)
    neg = -jnp.inf
    l_g = jnp.where(lane < N_GROUPS, lg, neg)
    m_g = jnp.max(l_g, axis=1, keepdims=True)
    g_top = jnp.min(jnp.where(l_g == m_g, lane, big), axis=1, keepdims=True)
    pg_top = 1.0 / jnp.sum(jnp.exp(l_g - m_g), axis=1, keepdims=True)
    e_lane = lane - N_GROUPS
    in_grp = jnp.logical_and(e_lane >= g_top * EXP_PER_GROUP, e_lane < (g_top + 1) * EXP_PER_GROUP)
    l_e = jnp.where(in_grp, lg, neg)
    m1 = jnp.max(l_e, axis=1, keepdims=True)
    i1 = jnp.min(jnp.where(l_e == m1, e_lane, big), axis=1, keepdims=True)
    l_e2 = jnp.where(e_lane == i1, neg, l_e)
    m2 = jnp.max(l_e2, axis=1, keepdims=True)
    i2 = jnp.min(jnp.where(l_e2 == m2, e_lane, big), axis=1, keepdims=True)
    e2 = jnp.exp(m2 - m1)
    w1 = pg_top / (1.0 + e2)
    w2 = pg_top * e2 / (1.0 + e2)
    ri_ref[...] = jnp.where(lane == 0, i1, jnp.where(lane == 1, i2, 0))
    rw_ref[...] = jnp.where(lane == 0, w1, jnp.where(lane == 1, w2, 0.0))


def _mix_kernel(og_ref, at_ref, gg_ref, gd_ref, x_ref, wbg_ref, wbd_ref, wo_ref, g_ref, b_ref,
                wrh_ref, wrl_ref, br_ref, x1_ref, ri_ref, rw_ref, *, alpha):
    bg = jnp.dot(og_ref[...], wbg_ref[...], preferred_element_type=F32)
    bd = jnp.dot(at_ref[...], wbd_ref[...], preferred_element_type=F32)
    merged = jax.nn.sigmoid(gg_ref[...]) * bg + jax.nn.sigmoid(gd_ref[...]) * bd
    mix = jnp.dot(merged.astype(BF16), wo_ref[...], preferred_element_type=F32)
    x1 = _ln(alpha * x_ref[...] + mix, g_ref[...], b_ref[...])
    x1_ref[...] = x1
    xh, xl = _hilo(x1)
    d = lambda p, q: jnp.dot(p, q, preferred_element_type=F32)
    lg = d(xh, wrh_ref[...]) + (d(xh, wrl_ref[...]) + d(xl, wrh_ref[...])) + br_ref[...]
    _route(lg, ri_ref, rw_ref)


def mix_stage(og, att, h, x, wbg, wbd, wo, g, b, wrh, wrl, br, layer, alpha):
    T, D = x.shape
    gw = GLA_HEADS * GLA_DV
    aw = ATT_HEADS * HEAD_DIM
    row = lambda w, c: pl.BlockSpec((TM, w), lambda i: (i, c))
    wsp = lambda r, c: pl.BlockSpec((None, r, c), lambda i: (layer, 0, 0))
    return pl.pallas_call(
        functools.partial(_mix_kernel, alpha=alpha),
        grid=(T // TM,),
        in_specs=[row(gw, 0), row(aw, 0), row(D, C_GG // D), row(D, C_GD // D), row(D, 0),
                  wsp(gw, D), wsp(aw, D), wsp(D, D), wsp(1, D), wsp(1, D),
                  wsp(D, LANES), wsp(D, LANES), wsp(1, LANES)],
        out_specs=[row(D, 0), row(LANES, 0), row(LANES, 0)],
        out_shape=[jax.ShapeDtypeStruct((T, D), F32), jax.ShapeDtypeStruct((T, LANES), I32),
                   jax.ShapeDtypeStruct((T, LANES), F32)],
        compiler_params=_cparams(("parallel",)),
        name="mix_stage",
    )(og, att, h, h, x, wbg, wbd, wo, g, b, wrh, wrl, br)


def _s_merge_kernel(og_ref, at_ref, gg_ref, gd_ref, wbg_ref, wbd_ref, o_ref):
    bg = _dot3(og_ref[...], wbg_ref[...])
    bd = _dot3(at_ref[...], wbd_ref[...])
    o_ref[...] = jax.nn.sigmoid(gg_ref[...]) * bg + jax.nn.sigmoid(gd_ref[...]) * bd


def s_merge(og, att, h, wbg, wbd, layer):
    R = og.shape[0]
    D = wbg.shape[-1]
    gw, aw = og.shape[1], att.shape[1]
    return pl.pallas_call(
        _s_merge_kernel,
        grid=(D // TK_S,),
        in_specs=[pl.BlockSpec((R, gw), lambda j: (0, 0)), pl.BlockSpec((R, aw), lambda j: (0, 0)),
                  pl.BlockSpec((R, TK_S), lambda j: (0, C_GG // TK_S + j)),
                  pl.BlockSpec((R, TK_S), lambda j: (0, C_GD // TK_S + j)),
                  pl.BlockSpec((None, gw, TK_S), lambda j: (layer, 0, j)),
                  pl.BlockSpec((None, aw, TK_S), lambda j: (layer, 0, j))],
        out_specs=pl.BlockSpec((R, TK_S), lambda j: (0, j)),
        out_shape=jax.ShapeDtypeStruct((R, D), F32),
        compiler_params=_cparams(("parallel",)),
        name="s_merge",
    )(og, att, h, h, wbg, wbd)


def _s_outproj_kernel(m_ref, x_ref, wo_ref, g_ref, b_ref, wr_ref, br_ref, x1_ref, ri_ref, rw_ref, acc,
                      *, alpha):
    k = pl.program_id(0)

    @pl.when(k == 0)
    def _():
        acc[...] = jnp.zeros_like(acc)

    acc[...] += _dot3(m_ref[...], wo_ref[...])

    @pl.when(k == pl.num_programs(0) - 1)
    def _():
        x1 = _ln(alpha * x_ref[...] + acc[...], g_ref[...], b_ref[...])
        x1_ref[...] = x1
        _route(_dot3(x1, wr_ref[...]) + br_ref[...], ri_ref, rw_ref)


def s_outproj(merged, x, wo, g, b, w_rt, br, layer, alpha):
    R, D = x.shape
    cst = lambda r, c: pl.BlockSpec((None, r, c), lambda k: (layer, 0, 0))
    full = lambda w: pl.BlockSpec((R, w), lambda k: (0, 0))
    return pl.pallas_call(
        functools.partial(_s_outproj_kernel, alpha=alpha),
        grid=(D // TK_S,),
        in_specs=[pl.BlockSpec((R, TK_S), lambda k: (0, k)), full(D),
                  pl.BlockSpec((None, TK_S, D), lambda k: (layer, k, 0)),
                  cst(1, D), cst(1, D), cst(D, LANES), cst(1, LANES)],
        out_specs=[full(D), full(LANES), full(LANES)],
        out_shape=[jax.ShapeDtypeStruct((R, D), F32), jax.ShapeDtypeStruct((R, LANES), I32),
                   jax.ShapeDtypeStruct((R, LANES), F32)],
        scratch_shapes=[pltpu.VMEM((R, D), F32)],
        compiler_params=_cparams(("arbitrary",)),
        name="s_outproj",
    )(merged, x, wo, g, b, w_rt, br)


def _moe_kernel(te_ref, nv_ref, tok_ref, pair_ref, w_ref, x_hbm, wg_ref, wu_ref, wd_ref, y_hbm,
                xbuf, ybuf, gsem, ssem, *, precise):
    i = pl.program_id(0)
    nv = nv_ref[i]
    n_chunks = (nv + DMA_UNROLL - 1) // DMA_UNROLL

    @pl.when(i == 0)
    def _():
        xbuf[...] = jnp.zeros_like(xbuf)
        n_real = y_hbm.shape[0] - DMA_UNROLL
        init = pltpu.make_async_copy(xbuf.at[pl.ds(0, DMA_UNROLL)], y_hbm.at[pl.ds(n_real, DMA_UNROLL)], ssem)
        init.start()
        init.wait()

    @pl.when(nv > 0)
    def _():
        def gcopy(r):
            return pltpu.make_async_copy(x_hbm.at[pl.ds(tok_ref[0, 0, r], 1)], xbuf.at[pl.ds(r, 1)], gsem)

        def scopy(r):
            return pltpu.make_async_copy(ybuf.at[pl.ds(r, 1)], y_hbm.at[pl.ds(pair_ref[0, 0, r], 1)], ssem)

        def chunked(fn):
            def body(c, carry):
                for u in range(DMA_UNROLL):
                    fn(c * DMA_UNROLL + u)
                return carry
            lax.fori_loop(0, n_chunks, body, 0)

        chunked(lambda r: gcopy(r).start())
        chunked(lambda r: gcopy(r).wait())
        mm = _dot3 if precise else _bdot
        x = xbuf[...]
        hg = mm(x, wg_ref[...])
        hu = mm(x, wu_ref[...])
        hid = (hg * jax.nn.sigmoid(hg)) * hu * w_ref[0]
        ybuf[...] = mm(hid, wd_ref[...])
        chunked(lambda r: scopy(r).start())
        chunked(lambda r: scopy(r).wait())


def moe_stage(x1, tile_e, tile_nv, slot_tok, slot_pair, slot_w, wg, wu, wd, layer, tm, precise):
    T, D = x1.shape
    n_tiles = tile_e.shape[0]
    F = wg.shape[-1]
    smem_blk = pl.BlockSpec((1, 1, tm), lambda i, te, nv: (i, 0, 0), memory_space=pltpu.SMEM)
    gs = pltpu.PrefetchScalarGridSpec(
        num_scalar_prefetch=2,
        grid=(n_tiles,),
        in_specs=[smem_blk, smem_blk,
                  pl.BlockSpec((1, tm, 1), lambda i, te, nv: (i, 0, 0)),
                  pl.BlockSpec(memory_space=pl.ANY),
                  pl.BlockSpec((None, None, D, F), lambda i, te, nv: (layer, te[i], 0, 0)),
                  pl.BlockSpec((None, None, D, F), lambda i, te, nv: (layer, te[i], 0, 0)),
                  pl.BlockSpec((None, None, F, D), lambda i, te, nv: (layer, te[i], 0, 0))],
        out_specs=pl.BlockSpec(memory_space=pl.ANY),
        scratch_shapes=[pltpu.VMEM((tm, D), F32), pltpu.VMEM((tm, D), F32),
                        pltpu.SemaphoreType.DMA, pltpu.SemaphoreType.DMA],
    )
    return pl.pallas_call(
        functools.partial(_moe_kernel, precise=precise),
        grid_spec=gs,
        out_shape=jax.ShapeDtypeStruct((TOPK_INNER * T + DMA_UNROLL, D), F32),
        compiler_params=_cparams(("arbitrary",)),
        name="moe_stage",
    )(tile_e, tile_nv, slot_tok, slot_pair, slot_w, x1, wg, wu, wd)


def moe_plan(route_i, route_w, tm):
    T = route_i.shape[0]
    P = TOPK_INNER * T
    n_tiles = -(-P // tm) + N_EXPERTS
    eid = route_i[:, :TOPK_INNER].T.reshape(P)
    wts = route_w[:, :TOPK_INNER].T.reshape(P)
    experts = jnp.arange(N_EXPERTS, dtype=I32)[None, :]
    onehot = (eid[:, None] == experts).astype(I32)
    counts = jnp.sum(onehot, axis=0)
    rank = jnp.sum((jnp.cumsum(onehot, axis=0) - onehot) * onehot, axis=1)
    tiles_per = (counts + tm - 1) // tm
    tile_end = jnp.cumsum(tiles_per)
    tile_start = tile_end - tiles_per
    n_used = tile_end[-1]
    ti = jnp.arange(n_tiles, dtype=I32)
    e_of = lambda t: jnp.sum((tile_end[None, :] <= t[:, None]).astype(I32), axis=1)
    used = ti < n_used
    tile_e = jnp.where(used, e_of(ti), e_of(jnp.broadcast_to(n_used - 1, (n_tiles,))))
    tile_e = jnp.clip(tile_e, 0, N_EXPERTS - 1)
    sel_e = tile_e[:, None] == experts
    local = ti - jnp.sum(jnp.where(sel_e, tile_start[None, :], 0), axis=1)
    cnt_e = jnp.sum(jnp.where(sel_e, counts[None, :], 0), axis=1)
    tile_nv = jnp.where(used, jnp.clip(cnt_e - local * tm, 0, tm), 0).astype(I32)
    slot = jnp.sum(onehot * tile_start[None, :], axis=1) * tm + rank
    spare = P + jnp.arange(n_tiles * tm, dtype=I32) % DMA_UNROLL
    slot_pair = spare.at[slot].set(jnp.arange(P, dtype=I32))
    is_real = slot_pair < P
    slot_tok = jnp.where(is_real, slot_pair % T, 0)
    slot_w = jnp.where(is_real, wts[jnp.where(is_real, slot_pair, 0)], 0.0)
    return (tile_e.astype(I32), tile_nv, slot_tok.reshape(n_tiles, 1, tm), slot_pair.reshape(n_tiles, 1, tm),
            slot_w.reshape(n_tiles, tm, 1))


def _ple_kernel(x1_ref, y0_ref, y1_ref, p_ref, g1_ref, b1_ref, wp_ref, wpg_ref, g2_ref, b2_ref,
                x3_ref, x3b_ref, *, alpha):
    ffn = y0_ref[...] + y1_ref[...]
    x2 = _ln(alpha * x1_ref[...] + ffn, g1_ref[...], b1_ref[...])
    ple = (jnp.dot(p_ref[...], wp_ref[...], preferred_element_type=F32)
           * jax.nn.sigmoid(jnp.dot(x2.astype(BF16), wpg_ref[...], preferred_element_type=F32)))
    x3 = _ln(alpha * x2 + ple, g2_ref[...], b2_ref[...])
    x3_ref[...] = x3
    x3b_ref[...] = x3.astype(BF16)


def ple_stage(x1, y, pb, g1, b1, wp, wpg, g2, b2, layer, alpha):
    T, D = x1.shape
    PD = pb.shape[1]
    nt = T // TM
    row = lambda w: pl.BlockSpec((TM, w), lambda i: (i, 0))
    wsp = lambda r, c: pl.BlockSpec((None, r, c), lambda i: (layer, 0, 0))
    return pl.pallas_call(
        functools.partial(_ple_kernel, alpha=alpha),
        grid=(nt,),
        in_specs=[row(D), row(D), pl.BlockSpec((TM, D), lambda i: (nt + i, 0)), row(PD),
                  wsp(1, D), wsp(1, D), wsp(PD, D), wsp(D, D), wsp(1, D), wsp(1, D)],
        out_specs=[row(D), row(D)],
        out_shape=[jax.ShapeDtypeStruct((T, D), F32), jax.ShapeDtypeStruct((T, D), BF16)],
        compiler_params=_cparams(("parallel",)),
        name="ple_stage",
    )(x1, y, y, pb, g1, b1, wp, wpg, g2, b2)


def _s_ple_kernel(x1_ref, y0_ref, y1_ref, p_ref, g1_ref, b1_ref, wp_ref, wpg_ref, g2_ref, b2_ref,
                  x3_ref, x2_s, x2k_s, pl_s, acc, *, alpha):
    k = pl.program_id(0)
    nk = x2k_s.shape[0]

    @pl.when(k == 0)
    def _():
        x2 = _ln(alpha * x1_ref[...] + (y0_ref[...] + y1_ref[...]), g1_ref[...], b1_ref[...])
        x2_s[...] = x2
        for kk in range(nk):
            x2k_s[kk] = x2[:, kk * TK_S:(kk + 1) * TK_S]
        pl_s[...] = _dot3(p_ref[...], wp_ref[...])
        acc[...] = jnp.zeros_like(acc)

    acc[...] += _dot3(x2k_s[k], wpg_ref[...])

    @pl.when(k == nk - 1)
    def _():
        x3_ref[...] = _ln(alpha * x2_s[...] + pl_s[...] * jax.nn.sigmoid(acc[...]), g2_ref[...], b2_ref[...])


def s_ple(x1, y, p, g1, b1, wp, wpg, g2, b2, layer, alpha):
    R, D = x1.shape
    PD = p.shape[1]
    cst = lambda r, c: pl.BlockSpec((None, r, c), lambda k: (layer, 0, 0))
    full = lambda w: pl.BlockSpec((R, w), lambda k: (0, 0))
    return pl.pallas_call(
        functools.partial(_s_ple_kernel, alpha=alpha),
        grid=(D // TK_S,),
        in_specs=[full(D), full(D), pl.BlockSpec((R, D), lambda k: (1, 0)), full(PD),
                  cst(1, D), cst(1, D), cst(PD, D), pl.BlockSpec((None, TK_S, D), lambda k: (layer, k, 0)),
                  cst(1, D), cst(1, D)],
        out_specs=full(D),
        out_shape=jax.ShapeDtypeStruct((R, D), F32),
        scratch_shapes=[pltpu.VMEM((R, D), F32), pltpu.VMEM((D // TK_S, R, TK_S), F32),
                        pltpu.VMEM((R, D), F32), pltpu.VMEM((R, D), F32)],
        compiler_params=_cparams(("arbitrary",)),
        name="s_ple",
    )(x1, y, y, p, g1, b1, wp, wpg, g2, b2)


def _rope_tables(pos, dh, period):
    rot = dh // 4
    half = rot // 2
    inv = ROPE_THETA ** (-jnp.arange(half, dtype=F32) / half)
    ang = pos.astype(F32)[:, None] * inv[None, :]
    cos, sin = jnp.cos(ang), jnp.sin(ang)
    T = pos.shape[0]
    z = lambda n: jnp.zeros((T, n), F32)
    c = jnp.concatenate([cos, cos, jnp.ones((T, period - rot), F32)], axis=1)
    s1 = jnp.concatenate([-sin, z(period - half)], axis=1)
    s2 = jnp.concatenate([z(half), sin, z(period - rot)], axis=1)
    rep = LANES // period
    return jnp.stack([jnp.tile(c, (1, rep)), jnp.tile(s1, (1, rep)), jnp.tile(s2, (1, rep))])


def kernel(x_prompt, x_sample, cache_k, cache_v, cache_kidx, state_gla, page_table, p_prompt, p_sample, ln_in_g, ln_in_b, w_in, w_decay, b_decay, gla_norm_g, w_branch_gla, w_branch_dsa, w_out, ln_mix_g, ln_mix_b, w_group, b_group, w_router, b_router, w_gate, w_up, w_down, ln_moe_g, ln_moe_b, w_ple, w_ple_gate, ln_ple_g, ln_ple_b):
    B, S, D = x_prompt.shape
    DB, DS, _ = x_sample.shape
    depth = w_in.shape[0]
    n_past = page_table.shape[1] * cache_k.shape[2]
    Tp = B * S
    tm_in = min(TM_IN, Tp)
    assert DS == 1 and DB == SUBLANES and S % GLA_CHUNK == 0 and S % TQ == 0 and Tp % tm_in == 0
    alpha = (2.0 * depth) ** 0.25

    w_in_t = jnp.swapaxes(w_in, 1, 2)
    ckidx_t = jnp.swapaxes(cache_kidx, 2, 3)
    w_small = jnp.concatenate([w_in[..., 6160:6224], w_in[..., 3072:3088], w_in[..., 6224:6240],
                               jnp.zeros((depth, D, LANES - 96), w_in.dtype)], axis=-1)
    w_small_b = w_small.astype(BF16)
    kw = GLA_HEADS * GLA_DK
    wdp32 = jnp.zeros((depth, LANES, kw), F32).at[:, L_GA:L_GA + GLA_RANK, :].set(w_decay)
    wdp = wdp32.astype(BF16)
    bd = b_decay.reshape(depth, 1, kw)
    gn = gla_norm_g.reshape(depth, 1, GLA_DV)
    wbg, wbd, wo = w_branch_gla.astype(BF16), w_branch_dsa.astype(BF16), w_out.astype(BF16)
    w_rt = jnp.concatenate([w_group, w_router, jnp.zeros((depth, D, LANES - N_GROUPS - N_EXPERTS), F32)], axis=-1)
    wrh = w_rt.astype(BF16)
    wrl = (w_rt - wrh.astype(F32)).astype(BF16)
    b_rt = jnp.concatenate([b_group, b_router, jnp.zeros((depth, LANES - N_GROUPS - N_EXPERTS), F32)],
                           axis=-1).reshape(depth, 1, LANES)
    wp, wpg = w_ple.astype(BF16), w_ple_gate.astype(BF16)
    r3 = lambda a: a.reshape(depth, 1, D)
    g_mix, b_mix, g_moe, b_moe, g_ple, b_ple = (r3(a) for a in (ln_mix_g, ln_mix_b, ln_moe_g, ln_moe_b,
                                                                  ln_ple_g, ln_ple_b))

    pos_p = jnp.tile(jnp.arange(S, dtype=I32), B)
    pos_s = jnp.full((DB,), n_past, I32)
    tp128, tp64 = _rope_tables(pos_p, HEAD_DIM, HEAD_DIM), _rope_tables(pos_p, IDX_DIM, IDX_DIM)
    ts128, ts64 = _rope_tables(pos_s, HEAD_DIM, HEAD_DIM), _rope_tables(pos_s, IDX_DIM, IDX_DIM)

    x, xb = ln_in(x_prompt.reshape(Tp, D), ln_in_g, ln_in_b, TM)
    xs, _ = ln_in(x_sample.reshape(DB, D), ln_in_g, ln_in_b, DB)
    outs = {k: [] for k in ("kp", "vp", "kip", "sp", "ks", "vs", "kis", "ss")}
    for l in range(depth):
        h = in_proj_main(xb, w_in_t, l, tm_in)
        hs = in_proj_small(xb, w_small_b, l, tm_in, False)
        ka, va, ki, qab, kab, vab, qib, kie, kio = rope_stage(h, hs, tp128, tp64, TM, BF16)
        og, s_p = gla_prompt(h, hs, wdp[l], bd[l], gn[l], B, S)
        att = dsa_prompt(qab, qib, hs, kab, vab, kie, kio, B, S)
        x1, route_i, route_w = mix_stage(og, att, h, x, wbg, wbd, wo, g_mix, b_mix, wrh, wrl, b_rt, l, alpha)
        y = moe_stage(x1, *moe_plan(route_i, route_w, TM), w_gate, w_up, w_down, l, TM, False)
        x, xb = ple_stage(x1, y, p_prompt[l].reshape(Tp, -1).astype(BF16), g_moe, b_moe, wp, wpg,
                          g_ple, b_ple, l, alpha)
        h_s = s_in_proj_main(xs, w_in_t, l)
        hs_s = in_proj_small(xs, w_small, l, DB, True)
        ka_s, va_s, ki_s, qa_s, _, _, qib_s, _, _ = rope_stage(h_s, hs_s, ts128, ts64, DB, F32)
        og_s, s_s = gla_sample(h_s, hs_s, wdp32[l], bd[l], gn[l], state_gla, l)
        at_s = dsa_sample(page_table, qib_s.reshape(DB, IDX_HEADS, IDX_DIM),
                          (hs_s[:, L_IW:L_IW + IDX_HEADS] * (IDX_HEADS ** -0.5)).reshape(DB, IDX_HEADS, 1),
                          ki_s.reshape(DB, 1, IDX_DIM),
                          qa_s.reshape(DB, ATT_HEADS, HEAD_DIM),
                          ka_s.reshape(DB, KV_HEADS, HEAD_DIM), va_s.reshape(DB, KV_HEADS, HEAD_DIM),
                          ckidx_t, cache_k, cache_v, l)
        merged_s = s_merge(og_s, at_s.reshape(DB, -1), h_s, w_branch_gla, w_branch_dsa, l)
        x1_s, ri_s, rw_s = s_outproj(merged_s, xs, w_out, g_mix, b_mix, w_rt, b_rt, l, alpha)
        y_s = moe_stage(x1_s, *moe_plan(ri_s, rw_s, DB), w_gate, w_up, w_down, l, DB, True)
        xs = s_ple(x1_s, y_s, p_sample[l].reshape(DB, -1), g_moe, b_moe, w_ple, w_ple_gate, g_ple, b_ple, l, alpha)
        for k, v in zip(outs, (ka, va, ki, s_p, ka_s, va_s, ki_s, s_s)):
            outs[k].append(v)

    st = {k: jnp.stack(v) for k, v in outs.items()}
    return (x.reshape(B, S, D), xs.reshape(DB, DS, D),
            st["kp"].reshape(depth, B, S, KV_HEADS, HEAD_DIM), st["vp"].reshape(depth, B, S, KV_HEADS, HEAD_DIM),
            st["kip"].reshape(depth, B, S, IDX_DIM), st["sp"],
            st["ks"].reshape(depth, DB, DS, KV_HEADS, HEAD_DIM), st["vs"].reshape(depth, DB, DS, KV_HEADS, HEAD_DIM),
            st["kis"].reshape(depth, DB, DS, IDX_DIM), st["ss"])
```

```python
import functools

import jax
import jax.numpy as jnp
from jax import lax
from jax.experimental import pallas as pl
from jax.experimental.pallas import tpu as pltpu

F32 = jnp.float32
BF16 = jnp.bfloat16
I32 = jnp.int32

GLA_HEADS = 4
GLA_DK = 128
GLA_DV = 256
GLA_RANK = 16
GLA_TAU = 16.0
ATT_HEADS = 8
KV_HEADS = 4
HEAD_DIM = 128
IDX_HEADS = 16
IDX_DIM = 64
TOPK_MAX = 256
ROPE_THETA = 500000.0
N_GROUPS = 4
EXP_PER_GROUP = 8
N_EXPERTS = N_GROUPS * EXP_PER_GROUP
TOPK_INNER = 2
LN_EPS = 1e-5

LANES = 128
SUBLANES = 8
TM_IN = 1024
TN_IN = 1024
TM = 256
GLA_CHUNK = 128
TQ = 128
KEY_BUCKET = 512
SCORE_PAGES = 8
RADIX_BITS = 4
DMA_UNROLL = 8
N_SPARE = 2 * DMA_UNROLL
TK_S = 512
VMEM_LIMIT = 56 * 1024 * 1024
INT_MIN = -(2 ** 31)

C_GQ, C_GK, C_GV, C_GR = 0, 512, 1024, 2048
C_AQ, C_AK, C_AV, C_IQ = 3072, 4096, 4608, 5120
C_GG, C_GD = 6144, 8192
N_MAIN = 10240
W_IN_TILE_OFFSETS = (0, 1024, 2048, 3088, 4112, 5136, 6240, 7264, 8288, 9312)
L_IK, L_GA, L_IW = 0, 64, 80


def _cparams(sem):
    return pltpu.CompilerParams(dimension_semantics=sem, vmem_limit_bytes=VMEM_LIMIT)


def _bdot(a, b):
    return jnp.dot(a.astype(BF16), b.astype(BF16), preferred_element_type=F32)


def _bdot_nt(a, b):
    return lax.dot_general(a.astype(BF16), b.astype(BF16), (((1,), (1,)), ((), ())),
                           preferred_element_type=F32)


def _hilo(x):
    hi = x.astype(BF16)
    return hi, (x - hi.astype(F32)).astype(BF16)


def _dot3(a, w):
    ah, al = _hilo(a)
    wh, wl = _hilo(w)
    d = lambda p, q: jnp.dot(p, q, preferred_element_type=F32)
    return d(ah, wh) + (d(al, wh) + d(ah, wl))


def _dot3_nt(a, wt):
    ah, al = _hilo(a)
    wh, wl = _hilo(wt)
    d = lambda p, q: lax.dot_general(p, q, (((1,), (1,)), ((), ())), preferred_element_type=F32)
    return d(ah, wh) + (d(al, wh) + d(ah, wl))


def _split3(x):
    hi = x.astype(BF16)
    r = x - hi.astype(F32)
    mid = r.astype(BF16)
    lo = (r - mid.astype(F32)).astype(BF16)
    return hi, mid, lo


def _ln(x, g, b):
    mu = jnp.mean(x, axis=-1, keepdims=True)
    xc = x - mu
    var = jnp.mean(xc * xc, axis=-1, keepdims=True)
    return xc * lax.rsqrt(var + LN_EPS) * g + b


def _col_to_row(col, n):
    eye = lax.broadcasted_iota(I32, (n, n), 0) == lax.broadcasted_iota(I32, (n, n), 1)
    return jnp.sum(jnp.where(eye, col, 0.0), axis=0, keepdims=True)


def _row_to_col(row, n):
    eye = lax.broadcasted_iota(I32, (n, n), 0) == lax.broadcasted_iota(I32, (n, n), 1)
    return jnp.sum(jnp.where(eye, row, 0.0), axis=1, keepdims=True)


def _sort_key(x):
    bits = lax.bitcast_convert_type(x, I32)
    return jnp.where(bits < 0, bits ^ jnp.int32(0x7FFFFFFF), bits)


def _ln_in_kernel(x_ref, g_ref, b_ref, o_ref, ob_ref):
    y = _ln(x_ref[...], g_ref[...], b_ref[...])
    o_ref[...] = y
    ob_ref[...] = y.astype(BF16)


def ln_in(x, g, b, tm):
    T, D = x.shape
    return pl.pallas_call(
        _ln_in_kernel,
        grid=(T // tm,),
        in_specs=[pl.BlockSpec((tm, D), lambda i: (i, 0)),
                  pl.BlockSpec((1, D), lambda i: (0, 0)),
                  pl.BlockSpec((1, D), lambda i: (0, 0))],
        out_specs=[pl.BlockSpec((tm, D), lambda i: (i, 0)),
                   pl.BlockSpec((tm, D), lambda i: (i, 0))],
        out_shape=[jax.ShapeDtypeStruct((T, D), F32), jax.ShapeDtypeStruct((T, D), BF16)],
        compiler_params=_cparams(("parallel",)),
        name="ln_in",
    )(x, g.reshape(1, D), b.reshape(1, D))


def _in_proj_kernel(off_ref, x_ref, wt_ref, o_ref, w_s):
    @pl.when(pl.program_id(1) == 0)
    def _():
        w_s[...] = wt_ref[0].T.astype(BF16)

    o_ref[...] = jnp.dot(x_ref[...], w_s[...], preferred_element_type=F32)


def _s_in_proj_kernel(off_ref, x_ref, wt_ref, o_ref):
    o_ref[...] = _dot3_nt(x_ref[...], wt_ref[0])


def _w_in_tile_spec(layer, D):
    return pl.BlockSpec((pl.Element(1), pl.Element(TN_IN), pl.Element(D)),
                        lambda j, *a: (layer, pl.multiple_of(a[-1][j], SUBLANES), 0))


def in_proj_main(xb, w_t, layer, tm):
    T, D = xb.shape
    offs = jnp.asarray(W_IN_TILE_OFFSETS, I32)
    gs = pltpu.PrefetchScalarGridSpec(
        num_scalar_prefetch=1,
        grid=(len(W_IN_TILE_OFFSETS), T // tm),
        in_specs=[pl.BlockSpec((tm, D), lambda j, i, off: (i, 0)), _w_in_tile_spec(layer, D)],
        out_specs=pl.BlockSpec((tm, TN_IN), lambda j, i, off: (i, j)),
        scratch_shapes=[pltpu.VMEM((D, TN_IN), BF16)],
    )
    return pl.pallas_call(
        _in_proj_kernel,
        grid_spec=gs,
        out_shape=jax.ShapeDtypeStruct((T, N_MAIN), F32),
        compiler_params=_cparams(("parallel", "arbitrary")),
        name="in_proj_main",
    )(offs, xb, w_t)


def s_in_proj_main(x, w_t, layer):
    R, D = x.shape
    offs = jnp.asarray(W_IN_TILE_OFFSETS, I32)
    gs = pltpu.PrefetchScalarGridSpec(
        num_scalar_prefetch=1,
        grid=(len(W_IN_TILE_OFFSETS),),
        in_specs=[pl.BlockSpec((R, D), lambda j, off: (0, 0)), _w_in_tile_spec(layer, D)],
        out_specs=pl.BlockSpec((R, TN_IN), lambda j, off: (0, j)),
    )
    return pl.pallas_call(
        _s_in_proj_kernel,
        grid_spec=gs,
        out_shape=jax.ShapeDtypeStruct((R, N_MAIN), F32),
        compiler_params=_cparams(("parallel",)),
        name="s_in_proj_main",
    )(offs, x, w_t)


def _mm_kernel(x_ref, w_ref, o_ref):
    o_ref[...] = jnp.dot(x_ref[...], w_ref[...], preferred_element_type=F32)


def _s_mm_kernel(x_ref, w_ref, o_ref):
    o_ref[...] = _dot3(x_ref[...], w_ref[...])


def in_proj_small(x, w, layer, tm, precise):
    T, D = x.shape
    N = w.shape[-1]
    return pl.pallas_call(
        _s_mm_kernel if precise else _mm_kernel,
        grid=(T // tm,),
        in_specs=[pl.BlockSpec((tm, D), lambda i: (i, 0)),
                  pl.BlockSpec((None, D, N), lambda i: (layer, 0, 0))],
        out_specs=pl.BlockSpec((tm, N), lambda i: (i, 0)),
        out_shape=jax.ShapeDtypeStruct((T, N), F32),
        compiler_params=_cparams(("parallel",)),
        name="in_proj_small",
    )(x, w)


def _rope_tile(x, c, s1, s2, shift):
    w = x.shape[-1]
    xm = pltpu.roll(x, w - shift, axis=1)
    xp = pltpu.roll(x, shift, axis=1)
    return x * c + xm * s1 + xp * s2


def _rope_kernel(aq_ref, ak_ref, av_ref, iq_ref, hs_ref, t128_ref, t64_ref,
                 ka_ref, va_ref, ki_ref, qa_ref, kab_ref, vab_ref, qib_ref, kie_ref, kio_ref):
    c1, s1a, s1b = t128_ref[0], t128_ref[1], t128_ref[2]
    c2, s2a, s2b = t64_ref[0], t64_ref[1], t64_ref[2]
    for hd in range(ATT_HEADS):
        sl = slice(hd * HEAD_DIM, (hd + 1) * HEAD_DIM)
        qa_ref[:, sl] = _rope_tile(aq_ref[:, sl], c1, s1a, s1b, HEAD_DIM // 8).astype(qa_ref.dtype)
    for hd in range(KV_HEADS):
        sl = slice(hd * HEAD_DIM, (hd + 1) * HEAD_DIM)
        kr = _rope_tile(ak_ref[:, sl], c1, s1a, s1b, HEAD_DIM // 8)
        ka_ref[:, sl] = kr
        kab_ref[:, sl] = kr.astype(BF16)
    v = av_ref[...]
    va_ref[...] = v
    if vab_ref.shape[0] == v.shape[0]:
        vab_ref[...] = v.astype(BF16)
    else:
        vab_ref[...] = v.T.astype(BF16)
    for pr in range(IDX_HEADS * IDX_DIM // LANES):
        sl = slice(pr * LANES, (pr + 1) * LANES)
        qr = _rope_tile(iq_ref[:, sl], c2, s2a, s2b, IDX_DIM // 8) * (IDX_DIM ** -0.5)
        qib_ref[:, sl] = qr.astype(qib_ref.dtype)
    kir = _rope_tile(hs_ref[...], c2, s2a, s2b, IDX_DIM // 8)
    ki_ref[...] = kir[:, :IDX_DIM]
    lane = lax.broadcasted_iota(I32, kir.shape, 1)
    ke = jnp.where(lane < IDX_DIM, kir, 0.0)
    kie_ref[...] = ke.astype(BF16)
    kio_ref[...] = pltpu.roll(ke, IDX_DIM, axis=1).astype(BF16)


def rope_stage(h, hs, t128, t64, tm, q_dtype, v_transposed):
    T = h.shape[0]
    aw, kw, iw = ATT_HEADS * HEAD_DIM, KV_HEADS * HEAD_DIM, IDX_HEADS * IDX_DIM
    row = lambda w, c: pl.BlockSpec((tm, w), lambda i: (i, c))
    tab = pl.BlockSpec((3, tm, LANES), lambda i: (0, i, 0))
    vb_spec = pl.BlockSpec((kw, tm), lambda i: (0, i)) if v_transposed else row(kw, 0)
    vb_shape = (kw, T) if v_transposed else (T, kw)
    return pl.pallas_call(
        _rope_kernel,
        grid=(T // tm,),
        in_specs=[row(aw, C_AQ // aw), row(kw, C_AK // kw), row(kw, C_AV // kw), row(iw, C_IQ // iw),
                  row(LANES, 0), tab, tab],
        out_specs=[row(kw, 0), row(kw, 0), row(IDX_DIM, 0), row(aw, 0), row(kw, 0), vb_spec,
                   row(iw, 0), row(LANES, 0), row(LANES, 0)],
        out_shape=[jax.ShapeDtypeStruct((T, kw), F32), jax.ShapeDtypeStruct((T, kw), F32),
                   jax.ShapeDtypeStruct((T, IDX_DIM), F32), jax.ShapeDtypeStruct((T, aw), q_dtype),
                   jax.ShapeDtypeStruct((T, kw), BF16), jax.ShapeDtypeStruct(vb_shape, BF16),
                   jax.ShapeDtypeStruct((T, iw), q_dtype), jax.ShapeDtypeStruct((T, LANES), BF16),
                   jax.ShapeDtypeStruct((T, LANES), BF16)],
        compiler_params=_cparams(("parallel",)),
        name="rope_stage",
    )(h, h, h, h, hs, t128, t64)


def _log_decay(hs, wdp_ref, bd_ref, precise=False):
    z = (_dot3 if precise else _bdot)(hs, wdp_ref[...]) + bd_ref[...]
    return (jnp.minimum(z, 0.0) - jnp.log1p(jnp.exp(-jnp.abs(z)))) * (1.0 / GLA_TAU)


def _gla_out(o, gn, gr):
    on = o * lax.rsqrt(jnp.mean(o * o, axis=-1, keepdims=True) + LN_EPS) * gn
    return on * (gr * jax.nn.sigmoid(gr))


def _gla_prompt_kernel(hq_ref, hk_ref, hv_ref, hr_ref, hs_ref, wdp_ref, bd_ref, gn_ref,
                       og_ref, sfin_ref, s_ref):
    c = pl.program_id(1)
    C = GLA_CHUNK

    @pl.when(c == 0)
    def _():
        s_ref[...] = jnp.zeros_like(s_ref)

    la_all = _log_decay(hs_ref[...], wdp_ref, bd_ref)
    r_i = lax.broadcasted_iota(I32, (C, C), 0)
    c_i = lax.broadcasted_iota(I32, (C, C), 1)
    causal = r_i >= c_i
    tri = jnp.where(causal, 1.0, 0.0).astype(BF16)
    gn = gn_ref[...]
    for hd in range(GLA_HEADS):
        ks = slice(hd * GLA_DK, (hd + 1) * GLA_DK)
        vs = slice(hd * GLA_DV, (hd + 1) * GLA_DV)
        la = la_all[:, ks]
        hi, mid, lo = _split3(la)
        b = (jnp.dot(tri, hi, preferred_element_type=F32)
             + jnp.dot(tri, mid, preferred_element_type=F32)
             + jnp.dot(tri, lo, preferred_element_type=F32))
        eb = jnp.exp(b)
        qe = hq_ref[:, ks] * (GLA_DK ** -0.5) * eb
        ke = hk_ref[:, ks] * jnp.exp(-b)
        v = hv_ref[:, vs]
        s_old = s_ref[hd]
        att = jnp.where(causal, _bdot_nt(qe, ke), 0.0)
        o = _bdot(att, v) + _bdot(qe, s_old)
        og_ref[:, vs] = _gla_out(o, gn, hr_ref[:, vs]).astype(BF16)
        eb_last = eb[C - 1:C, :]
        kdec = ke * eb_last
        upd = lax.dot_general(kdec.astype(BF16), v.astype(BF16), (((0,), (0,)), ((), ())),
                              preferred_element_type=F32)
        s_new = s_old * _row_to_col(jnp.broadcast_to(eb_last, (GLA_DK, GLA_DK)), GLA_DK) + upd
        s_ref[hd] = s_new

        @pl.when(c == pl.num_programs(1) - 1)
        def _():
            sfin_ref[0, hd] = s_new


def gla_prompt(h, hs, wdp, bd, gn, B, S):
    nC = S // GLA_CHUNK
    kw, vw = GLA_HEADS * GLA_DK, GLA_HEADS * GLA_DV
    row = lambda w, col: pl.BlockSpec((GLA_CHUNK, w), lambda b, c: (b * nC + c, col))
    cst = lambda shp: pl.BlockSpec(shp, lambda b, c: (0,) * len(shp))
    return pl.pallas_call(
        _gla_prompt_kernel,
        grid=(B, nC),
        in_specs=[row(kw, C_GQ // kw), row(kw, C_GK // kw), row(vw, C_GV // vw), row(vw, C_GR // vw),
                  row(LANES, 0), cst((LANES, kw)), cst((1, kw)), cst((1, GLA_DV))],
        out_specs=[row(vw, 0),
                   pl.BlockSpec((1, GLA_HEADS, GLA_DK, GLA_DV), lambda b, c: (b, 0, 0, 0))],
        out_shape=[jax.ShapeDtypeStruct((B * S, vw), BF16),
                   jax.ShapeDtypeStruct((B, GLA_HEADS, GLA_DK, GLA_DV), F32)],
        scratch_shapes=[pltpu.VMEM((GLA_HEADS, GLA_DK, GLA_DV), F32)],
        compiler_params=_cparams(("arbitrary", "arbitrary")),
        name="gla_prompt",
    )(h, h, h, h, hs, wdp, bd, gn)


def _gla_sample_kernel(hq_ref, hk_ref, hv_ref, hr_ref, hs_ref, wdp_ref, bd_ref, gn_ref, s0_ref,
                       og_ref, s1_ref):
    b = pl.program_id(0)
    la_all = _log_decay(hs_ref[...], wdp_ref, bd_ref, precise=True)
    gn = gn_ref[...]
    rowsel = lax.broadcasted_iota(I32, (hq_ref.shape[0], 1), 0) == b

    def pick(x):
        return jnp.sum(jnp.where(rowsel, x, 0.0), axis=0, keepdims=True)

    for hd in range(GLA_HEADS):
        ks = slice(hd * GLA_DK, (hd + 1) * GLA_DK)
        vs = slice(hd * GLA_DV, (hd + 1) * GLA_DV)
        a_row = jnp.exp(pick(la_all[:, ks]))
        q_row = pick(hq_ref[:, ks]) * (GLA_DK ** -0.5)
        k_row = pick(hk_ref[:, ks])
        v_row = pick(hv_ref[:, vs])
        bc = lambda r: _row_to_col(jnp.broadcast_to(r, (GLA_DK, GLA_DK)), GLA_DK)
        s_new = s0_ref[0, hd] * bc(a_row) + bc(k_row) * v_row
        s1_ref[0, hd] = s_new
        o = jnp.sum(bc(q_row) * s_new, axis=0, keepdims=True)
        og_ref[pl.ds(b, 1), vs] = _gla_out(o, gn, pick(hr_ref[:, vs]))


def gla_sample(h, hs, wdp, bd, gn, state, layer):
    DB = h.shape[0]
    kw, vw = GLA_HEADS * GLA_DK, GLA_HEADS * GLA_DV
    row = lambda w, col: pl.BlockSpec((DB, w), lambda b: (0, col))
    cst = lambda shp: pl.BlockSpec(shp, lambda b: (0,) * len(shp))
    return pl.pallas_call(
        _gla_sample_kernel,
        grid=(DB,),
        in_specs=[row(kw, C_GQ // kw), row(kw, C_GK // kw), row(vw, C_GV // vw), row(vw, C_GR // vw),
                  row(LANES, 0), cst((LANES, kw)), cst((1, kw)), cst((1, GLA_DV)),
                  pl.BlockSpec((None, 1, GLA_HEADS, GLA_DK, GLA_DV), lambda b: (layer, b, 0, 0, 0))],
        out_specs=[pl.BlockSpec((DB, vw), lambda b: (0, 0)),
                   pl.BlockSpec((1, GLA_HEADS, GLA_DK, GLA_DV), lambda b: (b, 0, 0, 0))],
        out_shape=[jax.ShapeDtypeStruct((DB, vw), F32),
                   jax.ShapeDtypeStruct((DB, GLA_HEADS, GLA_DK, GLA_DV), F32)],
        compiler_params=_cparams(("arbitrary",)),
        name="gla_sample",
    )(h, h, h, h, hs, wdp, bd, gn, state)


def _colreduce(x, red, pair, groups=8):
    n, C = x.shape
    x3 = x.reshape(n // SUBLANES, SUBLANES, C)
    rows = x3.shape[0] // groups
    parts = [red(x3[g * rows:(g + 1) * rows], axis=0) for g in range(groups)]
    while len(parts) > 1:
        parts = [pair(parts[j], parts[j + 1]) for j in range(0, len(parts), 2)]
    return red(parts[0], axis=0, keepdims=True)


def _colsum(x):
    return _colreduce(x, jnp.sum, jnp.add)


def _colmax(x):
    return _colreduce(x, jnp.max, jnp.maximum)


def _kth_threshold_t(key, topk):
    C = key.shape[1]

    def body(i, t):
        cand = t + jnp.left_shift(jnp.int32(1), 31 - i)
        cnt = _colsum(jnp.where(key >= cand, 1.0, 0.0))
        return jnp.where(cnt >= float(topk), cand, t)

    return lax.fori_loop(0, 32, body, jnp.full((1, C), INT_MIN, I32))


def _dsa_prompt_body(qb, nk, qa_ref, qi_ref, hs_ref, ka_ref, vt_ref, kie_ref, kio_ref, o_ref, topk):
    nt = (((1,), (1,)), ((), ()))
    hs_t = hs_ref[...].T
    acc = jnp.zeros((nk, TQ), F32)
    for hd in range(IDX_HEADS):
        pr = hd // 2
        q = qi_ref[:, pr * LANES:(pr + 1) * LANES]
        kk = kie_ref[:nk, :] if hd % 2 == 0 else kio_ref[:nk, :]
        s = lax.dot_general(kk, q, nt, preferred_element_type=F32)
        w = hs_t[L_IW + hd:L_IW + hd + 1, :] * (IDX_HEADS ** -0.5)
        acc = acc + w * jnp.maximum(s, 0.0)
    kpos = lax.broadcasted_iota(I32, (nk, TQ), 0)
    qpos = qb * TQ + lax.broadcasted_iota(I32, (nk, TQ), 1)
    vis = kpos <= qpos
    if nk > topk:
        key = jnp.where(vis, _sort_key(acc), INT_MIN)
        thr = _kth_threshold_t(key, topk)
        sel = jnp.logical_and(key >= thr, vis)
    else:
        sel = vis
    G = ATT_HEADS // KV_HEADS
    for n in range(KV_HEADS):
        ksl = slice(n * HEAD_DIM, (n + 1) * HEAD_DIM)
        kn = ka_ref[:nk, ksl]
        vt = vt_ref[ksl, :nk]
        for g in range(G):
            hsl = slice((n * G + g) * HEAD_DIM, (n * G + g + 1) * HEAD_DIM)
            s = lax.dot_general(kn, qa_ref[:, hsl], nt, preferred_element_type=F32) * (HEAD_DIM ** -0.5)
            s = jnp.where(sel, s, -jnp.inf)
            m = _colmax(s)
            p = jnp.exp(s - m)
            l = _colsum(p)
            o_t =jnp.dot(vt, p.astype(BF16), preferred_element_type=F32) / l
            o_ref[:, hsl] = o_t.T.astype(BF16)


def _dsa_prompt_kernel(qa_ref, qi_ref, hs_ref, ka_ref, va_ref, kie_ref, kio_ref, o_ref, *, topk, bucket):
    qb = pl.program_id(1)
    S = ka_ref.shape[0]
    for bi in range(S // bucket):
        @pl.when(qb // (bucket // TQ) == bi)
        def _():
            _dsa_prompt_body(qb, (bi + 1) * bucket, qa_ref, qi_ref, hs_ref, ka_ref, va_ref, kie_ref, kio_ref,
                             o_ref, topk)


def dsa_prompt(qab, qib, hs, kab, vab_t, kie, kio, B, S):
    nq = S // TQ
    aw, kw, iw = ATT_HEADS * HEAD_DIM, KV_HEADS * HEAD_DIM, IDX_HEADS * IDX_DIM
    topk = min(TOPK_MAX, S // 4)
    bucket = min(KEY_BUCKET, S)
    assert S % bucket == 0 and bucket % TQ == 0
    qrow = lambda w: pl.BlockSpec((TQ, w), lambda b, q: (b * nq + q, 0))
    seq = lambda w: pl.BlockSpec((S, w), lambda b, q: (b, 0))
    return pl.pallas_call(
        functools.partial(_dsa_prompt_kernel, topk=topk, bucket=bucket),
        grid=(B, nq),
        in_specs=[qrow(aw), qrow(iw), qrow(LANES), seq(kw), pl.BlockSpec((kw, S), lambda b, q: (0, b)),
                  seq(LANES), seq(LANES)],
        out_specs=qrow(aw),
        out_shape=jax.ShapeDtypeStruct((B * S, aw), BF16),
        compiler_params=_cparams(("parallel", "parallel")),
        name="dsa_prompt",
    )(qab, qib, hs, kab, vab_t, kie, kio)


def _dsa_sample_kernel(pt_ref, qi_ref, w_ref, kin_ref, qa_ref, kn_ref, vn_ref,
                       ckidx_hbm, ck_hbm, cv_hbm, o_ref,
                       kbuf, sc_ref, idxv_ref, idxs_ref, kg_ref, vg_ref, sem_i, sem_x, sem_k, sem_v,
                       *, layer, n_pages, page, topk):
    b = pl.program_id(0)
    NP, PG, K = n_pages, page, topk

    def icopy(p):
        return pltpu.make_async_copy(ckidx_hbm.at[layer, pt_ref[b, p]], kbuf.at[p], sem_i)

    lax.fori_loop(0, NP, lambda p, c: (icopy(p).start(), c)[1], 0)
    lax.fori_loop(0, NP, lambda p, c: (icopy(p).wait(), c)[1], 0)

    q = qi_ref[0]
    wcol = w_ref[0]

    def score_chunk(c, carry):
        pages = kbuf[pl.ds(c * SCORE_PAGES, SCORE_PAGES)]
        kc = jnp.concatenate([pages[j] for j in range(SCORE_PAGES)], axis=1)
        s = _dot3(q, kc)
        r = jnp.sum(wcol * jnp.maximum(s, 0.0), axis=0, keepdims=True)
        for j in range(SCORE_PAGES):
            sc_ref[pl.ds(c * SCORE_PAGES + j, 1), :] = r[:, j * PG:(j + 1) * PG]
        return carry

    lax.fori_loop(0, NP // SCORE_PAGES, score_chunk, 0)
    s_new = jnp.sum(q * kin_ref[0], axis=1, keepdims=True)
    s_new = jnp.sum(wcol * jnp.maximum(s_new, 0.0), axis=0, keepdims=True)

    key = _sort_key(sc_ref[...])
    key_new = _sort_key(s_new)

    def body(i, t):
        shift = 32 - RADIX_BITS * (i + 1)
        digit = jnp.zeros((1, 1), I32)
        for d in range(1, 2 ** RADIX_BITS):
            cand = t + jnp.left_shift(jnp.int32(d), shift)
            cnt = jnp.sum(jnp.sum(jnp.where(key >= cand, 1.0, 0.0), axis=0, keepdims=True), axis=1, keepdims=True)
            cnt = cnt + jnp.where(key_new >= cand, 1.0, 0.0)
            digit = digit + jnp.where(cnt >= float(K), 1, 0)
        return t + jnp.left_shift(digit, shift)

    thr = lax.fori_loop(0, 32 // RADIX_BITS, body, jnp.full((1, 1), INT_MIN, I32))
    gt = jnp.where(key > thr, 1.0, 0.0)
    eq = jnp.where(key == thr, 1.0, 0.0)
    n_gt = (jnp.sum(jnp.sum(gt, axis=1, keepdims=True), axis=0, keepdims=True)
            + jnp.where(key_new > thr, 1.0, 0.0))
    need = float(K) - n_gt

    r_p = lax.broadcasted_iota(I32, (PG, PG), 0)
    c_p = lax.broadcasted_iota(I32, (PG, PG), 1)
    ut = jnp.where(r_p <= c_p, 1.0, 0.0).astype(BF16)
    r_n = lax.broadcasted_iota(I32, (NP, NP), 0)
    c_n = lax.broadcasted_iota(I32, (NP, NP), 1)
    slt = jnp.where(c_n < r_n, 1.0, 0.0).astype(BF16)

    def prefix(m):
        cs = jnp.dot(m.astype(BF16), ut, preferred_element_type=F32)
        tot = cs[:, PG - 1:PG]
        off = jnp.dot(slt, jnp.broadcast_to(tot, (NP, PG)).astype(BF16), preferred_element_type=F32)[:, 0:1]
        return cs, off, tot

    cs_e, off_e, tot_e = prefix(eq)
    tie_rank = off_e + cs_e - eq
    sel = jnp.maximum(gt, jnp.where(tie_rank < need, eq, 0.0))
    n_eq_past = jnp.sum(tot_e, axis=0, keepdims=True)
    sel_new = jnp.logical_or(key_new > thr, jnp.logical_and(key_new == thr, n_eq_past < need))

    cs, off, tot = prefix(sel)
    n_past_sel = jnp.sum(tot, axis=0, keepdims=True)
    ci_row = _col_to_row(jnp.broadcast_to(off + tot, (NP, NP)), NP)
    off_row = _col_to_row(jnp.broadcast_to(off, (NP, NP)), NP)
    jcol = lax.broadcasted_iota(I32, (K, 1), 0).astype(F32)
    page_of = jnp.sum(jnp.where(ci_row <= jcol, 1.0, 0.0), axis=1, keepdims=True)
    lane_p = lax.broadcasted_iota(I32, (K, NP), 1).astype(F32)
    onehot = jnp.where(lane_p == page_of, 1.0, 0.0)
    off_j = jnp.sum(onehot * off_row, axis=1, keepdims=True)
    lr = jcol - off_j
    cs_row = jnp.dot(onehot.astype(BF16), cs.astype(BF16), preferred_element_type=F32)
    sel_row = jnp.dot(onehot.astype(BF16), sel.astype(BF16), preferred_element_type=F32)
    lane_o = lax.broadcasted_iota(I32, (K, PG), 1).astype(F32)
    hit = jnp.logical_and(cs_row - 1.0 == lr, sel_row > 0.5)
    off_of = jnp.sum(jnp.where(hit, lane_o, 0.0), axis=1, keepdims=True)
    valid_j = jcol < n_past_sel
    page_of = jnp.where(valid_j, page_of, 0.0)
    off_of = jnp.where(valid_j, off_of, 0.0)
    idxv_ref[...] = jnp.zeros_like(idxv_ref)
    nh = K // LANES
    for hh in range(nh):
        pg_r = _col_to_row(jnp.broadcast_to(page_of[hh * LANES:(hh + 1) * LANES], (LANES, LANES)), LANES)
        of_r = _col_to_row(jnp.broadcast_to(off_of[hh * LANES:(hh + 1) * LANES], (LANES, LANES)), LANES)
        idxv_ref[hh:hh + 1, :] = pg_r.astype(I32)
        idxv_ref[nh + hh:nh + hh + 1, :] = of_r.astype(I32)
    xcp = pltpu.make_async_copy(idxv_ref, idxs_ref, sem_x)
    xcp.start()
    xcp.wait()

    def kcopy(j):
        pg = idxs_ref[j // LANES, j % LANES]
        of = idxs_ref[nh + j // LANES, j % LANES]
        phys = pt_ref[b, pg]
        return (pltpu.make_async_copy(ck_hbm.at[layer, phys, of], kg_ref.at[j], sem_k),
                pltpu.make_async_copy(cv_hbm.at[layer, phys, of], vg_ref.at[j], sem_v))

    def start_j(c, carry):
        for u in range(DMA_UNROLL):
            ck, cv = kcopy(c * DMA_UNROLL + u)
            ck.start()
            cv.start()
        return carry

    def wait_j(c, carry):
        for u in range(DMA_UNROLL):
            ck, cv = kcopy(c * DMA_UNROLL + u)
            ck.wait()
            cv.wait()
        return carry

    lax.fori_loop(0, K // DMA_UNROLL, start_j, 0)
    lax.fori_loop(0, K // DMA_UNROLL, wait_j, 0)

    is_new = jnp.logical_and(jnp.logical_not(valid_j), sel_new)
    G = ATT_HEADS // KV_HEADS
    for n in range(KV_HEADS):
        kn = jnp.where(is_new, kn_ref[0, n:n + 1, :], kg_ref[:, n, :])
        vn = jnp.where(is_new, vn_ref[0, n:n + 1, :], vg_ref[:, n, :])
        qn = qa_ref[0, n * G:(n + 1) * G, :]
        s = _dot3_nt(qn, kn) * (HEAD_DIM ** -0.5)
        m = jnp.max(s, axis=1, keepdims=True)
        p = jnp.exp(s - m)
        l = jnp.sum(p, axis=1, keepdims=True)
        o_ref[0, n * G:(n + 1) * G, :] = _dot3(p, vn) / l


def dsa_sample(page_table, qi_s, w_s, kin_s, qa_s, kn_s, vn_s, cache_kidx_t, cache_k, cache_v, layer):
    DB, n_pages = page_table.shape
    page = cache_k.shape[2]
    L = n_pages * page + 1
    topk = min(TOPK_MAX, L // 4)
    assert topk % LANES == 0 and topk <= n_pages * page and n_pages % SCORE_PAGES == 0
    blk = lambda shp: pl.BlockSpec((1,) + shp, lambda b, pt: (b,) + (0,) * len(shp))
    any_spec = pl.BlockSpec(memory_space=pl.ANY)
    gs = pltpu.PrefetchScalarGridSpec(
        num_scalar_prefetch=1,
        grid=(DB,),
        in_specs=[blk((IDX_HEADS, IDX_DIM)), blk((IDX_HEADS, 1)), blk((1, IDX_DIM)),
                  blk((ATT_HEADS, HEAD_DIM)), blk((KV_HEADS, HEAD_DIM)), blk((KV_HEADS, HEAD_DIM)),
                  any_spec, any_spec, any_spec],
        out_specs=blk((ATT_HEADS, HEAD_DIM)),
        scratch_shapes=[pltpu.VMEM((n_pages, IDX_DIM, page), F32),
                        pltpu.VMEM((n_pages, page), F32),
                        pltpu.VMEM((SUBLANES, LANES), I32),
                        pltpu.SMEM((SUBLANES, LANES), I32),
                        pltpu.VMEM((topk, KV_HEADS, HEAD_DIM), F32),
                        pltpu.VMEM((topk, KV_HEADS, HEAD_DIM), F32),
                        pltpu.SemaphoreType.DMA, pltpu.SemaphoreType.DMA,
                        pltpu.SemaphoreType.DMA, pltpu.SemaphoreType.DMA],
    )
    return pl.pallas_call(
        functools.partial(_dsa_sample_kernel, layer=layer, n_pages=n_pages, page=page, topk=topk),
        grid_spec=gs,
        out_shape=jax.ShapeDtypeStruct((DB, ATT_HEADS, HEAD_DIM), F32),
        compiler_params=_cparams(("arbitrary",)),
        name="dsa_sample",
    )(page_table, qi_s, w_s, kin_s, qa_s, kn_s, vn_s, cache_kidx_t, cache_k, cache_v)


def _route(lg, ri_ref, rw_ref):
    lane = lax.broadcasted_iota(I32, lg.shape, 1)
    big = jnp.int32(1 << 20)
    neg = -jnp.inf
    l_g = jnp.where(lane < N_GROUPS, lg, neg)
    m_g = jnp.max(l_g, axis=1, keepdims=True)
    g_top = jnp.min(jnp.where(l_g == m_g, lane, big), axis=1, keepdims=True)
    pg_top = 1.0 / jnp.sum(jnp.exp(l_g - m_g), axis=1, keepdims=True)
    e_lane = lane - N_GROUPS
    in_grp = jnp.logical_and(e_lane >= g_top * EXP_PER_GROUP, e_lane < (g_top + 1) * EXP_PER_GROUP)
    l_e = jnp.where(in_grp, lg, neg)
    m1 = jnp.max(l_e, axis=1, keepdims=True)
    i1 = jnp.min(jnp.where(l_e == m1, e_lane, big), axis=1, keepdims=True)
    l_e2 = jnp.where(e_lane == i1, neg, l_e)
    m2 = jnp.max(l_e2, axis=1, keepdims=True)
    i2 = jnp.min(jnp.where(l_e2 == m2, e_lane, big), axis=1, keepdims=True)
    e2 = jnp.exp(m2 - m1)
    w1 = pg_top / (1.0 + e2)
    w2 = pg_top * e2 / (1.0 + e2)
    ri_ref[...] = jnp.where(lane == 0, i1, jnp.where(lane == 1, i2, 0))
    rw_ref[...] = jnp.where(lane == 0, w1, jnp.where(lane == 1, w2, 0.0))


def _mix_kernel(og_ref, at_ref, gg_ref, gd_ref, x_ref, wbg_ref, wbd_ref, wo_ref, g_ref, b_ref,
                wrh_ref, wrl_ref, br_ref, x1_ref, ri_ref, rw_ref, *, alpha):
    bg = jnp.dot(og_ref[...], wbg_ref[...], preferred_element_type=F32)
    bd = jnp.dot(at_ref[...], wbd_ref[...], preferred_element_type=F32)
    merged = jax.nn.sigmoid(gg_ref[...]) * bg + jax.nn.sigmoid(gd_ref[...]) * bd
    mix = jnp.dot(merged.astype(BF16), wo_ref[...], preferred_element_type=F32)
    x1 = _ln(alpha * x_ref[...] + mix, g_ref[...], b_ref[...])
    x1_ref[...] = x1
    xh, xl = _hilo(x1)
    d = lambda p, q: jnp.dot(p, q, preferred_element_type=F32)
    lg = d(xh, wrh_ref[...]) + (d(xh, wrl_ref[...]) + d(xl, wrh_ref[...])) + br_ref[...]
    _route(lg, ri_ref, rw_ref)


def mix_stage(og, att, h, x, wbg, wbd, wo, g, b, wrh, wrl, br, layer, alpha):
    T, D = x.shape
    gw = GLA_HEADS * GLA_DV
    aw = ATT_HEADS * HEAD_DIM
    row = lambda w, c: pl.BlockSpec((TM, w), lambda i: (i, c))
    wsp = lambda r, c: pl.BlockSpec((None, r, c), lambda i: (layer, 0, 0))
    return pl.pallas_call(
        functools.partial(_mix_kernel, alpha=alpha),
        grid=(T // TM,),
        in_specs=[row(gw, 0), row(aw, 0), row(D, C_GG // D), row(D, C_GD // D), row(D, 0),
                  wsp(gw, D), wsp(aw, D), wsp(D, D), wsp(1, D), wsp(1, D),
                  wsp(D, LANES), wsp(D, LANES), wsp(1, LANES)],
        out_specs=[row(D, 0), row(LANES, 0), row(LANES, 0)],
        out_shape=[jax.ShapeDtypeStruct((T, D), F32), jax.ShapeDtypeStruct((T, LANES), I32),
                   jax.ShapeDtypeStruct((T, LANES), F32)],
        compiler_params=_cparams(("parallel",)),
        name="mix_stage",
    )(og, att, h, h, x, wbg, wbd, wo, g, b, wrh, wrl, br)


def _s_merge_kernel(og_ref, at_ref, gg_ref, gd_ref, wbg_ref, wbd_ref, o_ref):
    bg = _dot3(og_ref[...], wbg_ref[...])
    bd = _dot3(at_ref[...], wbd_ref[...])
    o_ref[...] = jax.nn.sigmoid(gg_ref[...]) * bg + jax.nn.sigmoid(gd_ref[...]) * bd


def s_merge(og, att, h, wbg, wbd, layer):
    R = og.shape[0]
    D = wbg.shape[-1]
    gw, aw = og.shape[1], att.shape[1]
    return pl.pallas_call(
        _s_merge_kernel,
        grid=(D // TK_S,),
        in_specs=[pl.BlockSpec((R, gw), lambda j: (0, 0)), pl.BlockSpec((R, aw), lambda j: (0, 0)),
                  pl.BlockSpec((R, TK_S), lambda j: (0, C_GG // TK_S + j)),
                  pl.BlockSpec((R, TK_S), lambda j: (0, C_GD // TK_S + j)),
                  pl.BlockSpec((None, gw, TK_S), lambda j: (layer, 0, j)),
                  pl.BlockSpec((None, aw, TK_S), lambda j: (layer, 0, j))],
        out_specs=pl.BlockSpec((R, TK_S), lambda j: (0, j)),
        out_shape=jax.ShapeDtypeStruct((R, D), F32),
        compiler_params=_cparams(("parallel",)),
        name="s_merge",
    )(og, att, h, h, wbg, wbd)


def _s_outproj_kernel(m_ref, x_ref, wo_ref, g_ref, b_ref, wr_ref, br_ref, x1_ref, ri_ref, rw_ref, acc,
                      *, alpha):
    k = pl.program_id(0)

    @pl.when(k == 0)
    def _():
        acc[...] = jnp.zeros_like(acc)

    acc[...] += _dot3(m_ref[...], wo_ref[...])

    @pl.when(k == pl.num_programs(0) - 1)
    def _():
        x1 = _ln(alpha * x_ref[...] + acc[...], g_ref[...], b_ref[...])
        x1_ref[...] = x1
        _route(_dot3(x1, wr_ref[...]) + br_ref[...], ri_ref, rw_ref)


def s_outproj(merged, x, wo, g, b, w_rt, br, layer, alpha):
    R, D = x.shape
    cst = lambda r, c: pl.BlockSpec((None, r, c), lambda k: (layer, 0, 0))
    full = lambda w: pl.BlockSpec((R, w), lambda k: (0, 0))
    return pl.pallas_call(
        functools.partial(_s_outproj_kernel, alpha=alpha),
        grid=(D // TK_S,),
        in_specs=[pl.BlockSpec((R, TK_S), lambda k: (0, k)), full(D),
                  pl.BlockSpec((None, TK_S, D), lambda k: (layer, k, 0)),
                  cst(1, D), cst(1, D), cst(D, LANES), cst(1, LANES)],
        out_specs=[full(D), full(LANES), full(LANES)],
        out_shape=[jax.ShapeDtypeStruct((R, D), F32), jax.ShapeDtypeStruct((R, LANES), I32),
                   jax.ShapeDtypeStruct((R, LANES), F32)],
        scratch_shapes=[pltpu.VMEM((R, D), F32)],
        compiler_params=_cparams(("arbitrary",)),
        name="s_outproj",
    )(merged, x, wo, g, b, w_rt, br)


def _moe_kernel(te_ref, nv_ref, tok_ref, tok_nx_ref, pair_ref, pair_pv_ref, x_hbm, wg_ref, wu_ref, wd_ref, y_hbm,
                xbuf, ybuf, gsem, ssem, *, precise):
    i = pl.program_id(0)
    n = pl.num_programs(0)
    cur = i % 2
    nv = nv_ref[i]
    nv_next = jnp.where(i + 1 < n, nv_ref[jnp.minimum(i + 1, n - 1)], 0)
    nv_prev = jnp.where(i > 0, nv_ref[jnp.maximum(i - 1, 0)], 0)

    def chunked(count, fn):
        def body(c, carry):
            for u in range(DMA_UNROLL):
                fn(c * DMA_UNROLL + u)
            return carry
        lax.fori_loop(0, (count + DMA_UNROLL - 1) // DMA_UNROLL, body, 0)

    def gcopy(ids_ref, buf, r):
        return pltpu.make_async_copy(x_hbm.at[pl.ds(ids_ref[0, 0, r], 1)], xbuf.at[buf, pl.ds(r, 1)], gsem.at[buf])

    def scopy(ids_ref, buf, r):
        return pltpu.make_async_copy(ybuf.at[buf, pl.ds(r, 1)], y_hbm.at[pl.ds(ids_ref[0, 0, r], 1)], ssem.at[buf])

    @pl.when(i == 0)
    def _():
        xbuf[...] = jnp.zeros_like(xbuf)
        n_real = y_hbm.shape[0] - N_SPARE
        for k in range(N_SPARE // DMA_UNROLL):
            init = pltpu.make_async_copy(xbuf.at[0, pl.ds(0, DMA_UNROLL)],
                                         y_hbm.at[pl.ds(n_real + k * DMA_UNROLL, DMA_UNROLL)], ssem.at[0])
            init.start()
            init.wait()
        chunked(nv, lambda r: gcopy(tok_ref, 0, r).start())

    chunked(nv_next, lambda r: gcopy(tok_nx_ref, 1 - cur, r).start())

    @pl.when(nv > 0)
    def _():
        chunked(nv, lambda r: gcopy(tok_ref, cur, r).wait())
        mm = _dot3 if precise else _bdot
        x = xbuf[cur]
        hg = mm(x, wg_ref[...])
        hu = mm(x, wu_ref[...])
        hid = (hg * jax.nn.sigmoid(hg)) * hu
        ybuf[cur] = mm(hid, wd_ref[...])
        chunked(nv, lambda r: scopy(pair_ref, cur, r).start())

    chunked(nv_prev, lambda r: scopy(pair_pv_ref, 1 - cur, r).wait())

    @pl.when(i == n - 1)
    def _():
        chunked(nv, lambda r: scopy(pair_ref, cur, r).wait())


def moe_stage(x1, tile_e, tile_nv, slot_tok, slot_pair, wg, wu, wd, layer, tm, precise):
    T, D = x1.shape
    n_tiles = tile_e.shape[0]
    F = wg.shape[-1]
    ids = lambda shift: pl.BlockSpec((1, 1, tm), lambda i, te, nv: (jnp.clip(i + shift, 0, n_tiles - 1), 0, 0),
                                     memory_space=pltpu.SMEM)
    gs = pltpu.PrefetchScalarGridSpec(
        num_scalar_prefetch=2,
        grid=(n_tiles,),
        in_specs=[ids(0), ids(1), ids(0), ids(-1),
                  pl.BlockSpec(memory_space=pl.ANY),
                  pl.BlockSpec((None, None, D, F), lambda i, te, nv: (layer, te[i], 0, 0)),
                  pl.BlockSpec((None, None, D, F), lambda i, te, nv: (layer, te[i], 0, 0)),
                  pl.BlockSpec((None, None, F, D), lambda i, te, nv: (layer, te[i], 0, 0))],
        out_specs=pl.BlockSpec(memory_space=pl.ANY),
        scratch_shapes=[pltpu.VMEM((2, tm, D), F32), pltpu.VMEM((2, tm, D), F32),
                        pltpu.SemaphoreType.DMA((2,)), pltpu.SemaphoreType.DMA((2,))],
    )
    return pl.pallas_call(
        functools.partial(_moe_kernel, precise=precise),
        grid_spec=gs,
        out_shape=jax.ShapeDtypeStruct((TOPK_INNER * T + N_SPARE, D), F32),
        compiler_params=_cparams(("arbitrary",)),
        name="moe_stage",
    )(tile_e, tile_nv, slot_tok, slot_tok, slot_pair, slot_pair, x1, wg, wu, wd)


def moe_plan(route_i, tm):
    T = route_i.shape[0]
    P = TOPK_INNER * T
    n_tiles = -(-P // tm) + N_EXPERTS
    eid = route_i[:, :TOPK_INNER].T.reshape(P)
    experts = jnp.arange(N_EXPERTS, dtype=I32)[None, :]
    onehot = (eid[:, None] == experts).astype(I32)
    counts = jnp.sum(onehot, axis=0)
    rank = jnp.sum((jnp.cumsum(onehot, axis=0) - onehot) * onehot, axis=1)
    tiles_per = (counts + tm - 1) // tm
    tile_end = jnp.cumsum(tiles_per)
    tile_start = tile_end - tiles_per
    n_used = tile_end[-1]
    ti = jnp.arange(n_tiles, dtype=I32)
    e_of = lambda t: jnp.sum((tile_end[None, :] <= t[:, None]).astype(I32), axis=1)
    used = ti < n_used
    tile_e = jnp.where(used, e_of(ti), e_of(jnp.broadcast_to(n_used - 1, (n_tiles,))))
    tile_e = jnp.clip(tile_e, 0, N_EXPERTS - 1)
    sel_e = tile_e[:, None] == experts
    local = ti - jnp.sum(jnp.where(sel_e, tile_start[None, :], 0), axis=1)
    cnt_e = jnp.sum(jnp.where(sel_e, counts[None, :], 0), axis=1)
    tile_nv = jnp.where(used, jnp.clip(cnt_e - local * tm, 0, tm), 0).astype(I32)
    slot = jnp.sum(onehot * tile_start[None, :], axis=1) * tm + rank
    s_idx = jnp.arange(n_tiles * tm, dtype=I32)
    spare = P + ((s_idx // tm) % 2) * DMA_UNROLL + s_idx % DMA_UNROLL
    slot_pair = spare.at[slot].set(jnp.arange(P, dtype=I32))
    slot_tok = jnp.where(slot_pair < P, slot_pair % T, 0)
    return tile_e.astype(I32), tile_nv, slot_tok.reshape(n_tiles, 1, tm), slot_pair.reshape(n_tiles, 1, tm)


def _ffn_sum(y0_ref, y1_ref, rw_ref):
    rw = rw_ref[...]
    return rw[:, 0:1] * y0_ref[...] + rw[:, 1:2] * y1_ref[...]


def _ple_kernel(x1_ref, y0_ref, y1_ref, rw_ref, p_ref, g1_ref, b1_ref, wp_ref, wpg_ref, g2_ref, b2_ref,
                x3_ref, x3b_ref, *, alpha):
    ffn = _ffn_sum(y0_ref, y1_ref, rw_ref)
    x2 = _ln(alpha * x1_ref[...] + ffn, g1_ref[...], b1_ref[...])
    ple = (jnp.dot(p_ref[...], wp_ref[...], preferred_element_type=F32)
           * jax.nn.sigmoid(jnp.dot(x2.astype(BF16), wpg_ref[...], preferred_element_type=F32)))
    x3 = _ln(alpha * x2 + ple, g2_ref[...], b2_ref[...])
    x3_ref[...] = x3
    x3b_ref[...] = x3.astype(BF16)


def ple_stage(x1, y, rw, pb, g1, b1, wp, wpg, g2, b2, layer, alpha):
    T, D = x1.shape
    PD = pb.shape[1]
    nt = T // TM
    row = lambda w: pl.BlockSpec((TM, w), lambda i: (i, 0))
    wsp = lambda r, c: pl.BlockSpec((None, r, c), lambda i: (layer, 0, 0))
    return pl.pallas_call(
        functools.partial(_ple_kernel, alpha=alpha),
        grid=(nt,),
        in_specs=[row(D), row(D), pl.BlockSpec((TM, D), lambda i: (nt + i, 0)), row(LANES), row(PD),
                  wsp(1, D), wsp(1, D), wsp(PD, D), wsp(D, D), wsp(1, D), wsp(1, D)],
        out_specs=[row(D), row(D)],
        out_shape=[jax.ShapeDtypeStruct((T, D), F32), jax.ShapeDtypeStruct((T, D), BF16)],
        compiler_params=_cparams(("parallel",)),
        name="ple_stage",
    )(x1, y, y, rw, pb, g1, b1, wp, wpg, g2, b2)


def _s_ple_kernel(x1_ref, y0_ref, y1_ref, rw_ref, p_ref, g1_ref, b1_ref, wp_ref, wpg_ref, g2_ref, b2_ref,
                  x3_ref, x2_s, x2k_s, pl_s, acc, *, alpha):
    k = pl.program_id(0)
    nk = x2k_s.shape[0]

    @pl.when(k == 0)
    def _():
        x2 = _ln(alpha * x1_ref[...] + _ffn_sum(y0_ref, y1_ref, rw_ref), g1_ref[...], b1_ref[...])
        x2_s[...] = x2
        for kk in range(nk):
            x2k_s[kk] = x2[:, kk * TK_S:(kk + 1) * TK_S]
        pl_s[...] = _dot3(p_ref[...], wp_ref[...])
        acc[...] = jnp.zeros_like(acc)

    acc[...] += _dot3(x2k_s[k], wpg_ref[...])

    @pl.when(k == nk - 1)
    def _():
        x3_ref[...] = _ln(alpha * x2_s[...] + pl_s[...] * jax.nn.sigmoid(acc[...]), g2_ref[...], b2_ref[...])


def s_ple(x1, y, rw, p, g1, b1, wp, wpg, g2, b2, layer, alpha):
    R, D = x1.shape
    PD = p.shape[1]
    cst = lambda r, c: pl.BlockSpec((None, r, c), lambda k: (layer, 0, 0))
    full = lambda w: pl.BlockSpec((R, w), lambda k: (0, 0))
    return pl.pallas_call(
        functools.partial(_s_ple_kernel, alpha=alpha),
        grid=(D // TK_S,),
        in_specs=[full(D), full(D), pl.BlockSpec((R, D), lambda k: (1, 0)), full(LANES), full(PD),
                  cst(1, D), cst(1, D), cst(PD, D), pl.BlockSpec((None, TK_S, D), lambda k: (layer, k, 0)),
                  cst(1, D), cst(1, D)],
        out_specs=full(D),
        out_shape=jax.ShapeDtypeStruct((R, D), F32),
        scratch_shapes=[pltpu.VMEM((R, D), F32), pltpu.VMEM((D // TK_S, R, TK_S), F32),
                        pltpu.VMEM((R, D), F32), pltpu.VMEM((R, D), F32)],
        compiler_params=_cparams(("arbitrary",)),
        name="s_ple",
    )(x1, y, y, rw, p, g1, b1, wp, wpg, g2, b2)


def _rope_tables(pos, dh, period):
    rot = dh // 4
    half = rot // 2
    inv = ROPE_THETA ** (-jnp.arange(half, dtype=F32) / half)
    ang = pos.astype(F32)[:, None] * inv[None, :]
    cos, sin = jnp.cos(ang), jnp.sin(ang)
    T = pos.shape[0]
    z = lambda n: jnp.zeros((T, n), F32)
    c = jnp.concatenate([cos, cos, jnp.ones((T, period - rot), F32)], axis=1)
    s1 = jnp.concatenate([-sin, z(period - half)], axis=1)
    s2 = jnp.concatenate([z(half), sin, z(period - rot)], axis=1)
    rep = LANES // period
    return jnp.stack([jnp.tile(c, (1, rep)), jnp.tile(s1, (1, rep)), jnp.tile(s2, (1, rep))])


def kernel(x_prompt, x_sample, cache_k, cache_v, cache_kidx, state_gla, page_table, p_prompt, p_sample, ln_in_g, ln_in_b, w_in, w_decay, b_decay, gla_norm_g, w_branch_gla, w_branch_dsa, w_out, ln_mix_g, ln_mix_b, w_group, b_group, w_router, b_router, w_gate, w_up, w_down, ln_moe_g, ln_moe_b, w_ple, w_ple_gate, ln_ple_g, ln_ple_b):
    B, S, D = x_prompt.shape
    DB, DS, _ = x_sample.shape
    depth = w_in.shape[0]
    n_past = page_table.shape[1] * cache_k.shape[2]
    Tp = B * S
    tm_in = min(TM_IN, Tp)
    assert DS == 1 and DB == SUBLANES and S % GLA_CHUNK == 0 and S % TQ == 0 and Tp % tm_in == 0
    alpha = (2.0 * depth) ** 0.25

    w_in_t = jnp.swapaxes(w_in, 1, 2)
    ckidx_t = jnp.swapaxes(cache_kidx, 2, 3)
    w_small = jnp.concatenate([w_in[..., 6160:6224], w_in[..., 3072:3088], w_in[..., 6224:6240],
                               jnp.zeros((depth, D, LANES - 96), w_in.dtype)], axis=-1)
    w_small_b = w_small.astype(BF16)
    kw = GLA_HEADS * GLA_DK
    wdp32 = jnp.zeros((depth, LANES, kw), F32).at[:, L_GA:L_GA + GLA_RANK, :].set(w_decay)
    wdp = wdp32.astype(BF16)
    bd = b_decay.reshape(depth, 1, kw)
    gn = gla_norm_g.reshape(depth, 1, GLA_DV)
    wbg, wbd, wo = w_branch_gla.astype(BF16), w_branch_dsa.astype(BF16), w_out.astype(BF16)
    w_rt = jnp.concatenate([w_group, w_router, jnp.zeros((depth, D, LANES - N_GROUPS - N_EXPERTS), F32)], axis=-1)
    wrh = w_rt.astype(BF16)
    wrl = (w_rt - wrh.astype(F32)).astype(BF16)
    b_rt = jnp.concatenate([b_group, b_router, jnp.zeros((depth, LANES - N_GROUPS - N_EXPERTS), F32)],
                           axis=-1).reshape(depth, 1, LANES)
    wp, wpg = w_ple.astype(BF16), w_ple_gate.astype(BF16)
    r3 = lambda a: a.reshape(depth, 1, D)
    g_mix, b_mix, g_moe, b_moe, g_ple, b_ple = (r3(a) for a in (ln_mix_g, ln_mix_b, ln_moe_g, ln_moe_b,
                                                                  ln_ple_g, ln_ple_b))

    pos_p = jnp.tile(jnp.arange(S, dtype=I32), B)
    pos_s = jnp.full((DB,), n_past, I32)
    tp128, tp64 = _rope_tables(pos_p, HEAD_DIM, HEAD_DIM), _rope_tables(pos_p, IDX_DIM, IDX_DIM)
    ts128, ts64 = _rope_tables(pos_s, HEAD_DIM, HEAD_DIM), _rope_tables(pos_s, IDX_DIM, IDX_DIM)

    x, xb = ln_in(x_prompt.reshape(Tp, D), ln_in_g, ln_in_b, TM)
    xs, _ = ln_in(x_sample.reshape(DB, D), ln_in_g, ln_in_b, DB)
    outs = {k: [] for k in ("kp", "vp", "kip", "sp", "ks", "vs", "kis", "ss")}
    for l in range(depth):
        h = in_proj_main(xb, w_in_t, l, tm_in)
        hs = in_proj_small(xb, w_small_b, l, tm_in, False)
        ka, va, ki, qab, kab, vab_t, qib, kie, kio = rope_stage(h, hs, tp128, tp64, TM, BF16, True)
        og, s_p = gla_prompt(h, hs, wdp[l], bd[l], gn[l], B, S)
        att = dsa_prompt(qab, qib, hs, kab, vab_t, kie, kio, B, S)
        x1, route_i, route_w = mix_stage(og, att, h, x, wbg, wbd, wo, g_mix, b_mix, wrh, wrl, b_rt, l, alpha)
        y = moe_stage(x1, *moe_plan(route_i, TM), w_gate, w_up, w_down, l, TM, False)
        x, xb = ple_stage(x1, y, route_w, p_prompt[l].reshape(Tp, -1).astype(BF16), g_moe, b_moe, wp, wpg,
                          g_ple, b_ple, l, alpha)
        h_s = s_in_proj_main(xs, w_in_t, l)
        hs_s = in_proj_small(xs, w_small, l, DB, True)
        ka_s, va_s, ki_s, qa_s, _, _, qib_s, _, _ = rope_stage(h_s, hs_s, ts128, ts64, DB, F32, False)
        og_s, s_s = gla_sample(h_s, hs_s, wdp32[l], bd[l], gn[l], state_gla, l)
        at_s = dsa_sample(page_table, qib_s.reshape(DB, IDX_HEADS, IDX_DIM),
                          (hs_s[:, L_IW:L_IW + IDX_HEADS] * (IDX_HEADS ** -0.5)).reshape(DB, IDX_HEADS, 1),
                          ki_s.reshape(DB, 1, IDX_DIM),
                          qa_s.reshape(DB, ATT_HEADS, HEAD_DIM),
                          ka_s.reshape(DB, KV_HEADS, HEAD_DIM), va_s.reshape(DB, KV_HEADS, HEAD_DIM),
                          ckidx_t, cache_k, cache_v, l)
        merged_s = s_merge(og_s, at_s.reshape(DB, -1), h_s, w_branch_gla, w_branch_dsa, l)
        x1_s, ri_s, rw_s = s_outproj(merged_s, xs, w_out, g_mix, b_mix, w_rt, b_rt, l, alpha)
        y_s = moe_stage(x1_s, *moe_plan(ri_s, DB), w_gate, w_up, w_down, l, DB, True)
        xs = s_ple(x1_s, y_s, rw_s, p_sample[l].reshape(DB, -1), g_moe, b_moe, w_ple, w_ple_gate, g_ple, b_ple, l, alpha)
        for k, v in zip(outs, (ka, va, ki, s_p, ka_s, va_s, ki_s, s_s)):
            outs[k].append(v)

    st = {k: jnp.stack(v) for k, v in outs.items()}
    return (x.reshape(B, S, D), xs.reshape(DB, DS, D),
            st["kp"].reshape(depth, B, S, KV_HEADS, HEAD_DIM), st["vp"].reshape(depth, B, S, KV_HEADS, HEAD_DIM),
            st["kip"].reshape(depth, B, S, IDX_DIM), st["sp"],
            st["ks"].reshape(depth, DB, DS, KV_HEADS, HEAD_DIM), st["vs"].reshape(depth, DB, DS, KV_HEADS, HEAD_DIM),
            st["kis"].reshape(depth, DB, DS, IDX_DIM), st["ss"])
```

```python
import functools

import jax
import jax.numpy as jnp
from jax import lax
from jax.experimental import pallas as pl
from jax.experimental.pallas import tpu as pltpu

F32 = jnp.float32
BF16 = jnp.bfloat16
I32 = jnp.int32

GLA_HEADS = 4
GLA_DK = 128
GLA_DV = 256
GLA_RANK = 16
GLA_TAU = 16.0
ATT_HEADS = 8
KV_HEADS = 4
HEAD_DIM = 128
IDX_HEADS = 16
IDX_DIM = 64
TOPK_MAX = 256
ROPE_THETA = 500000.0
N_GROUPS = 4
EXP_PER_GROUP = 8
N_EXPERTS = N_GROUPS * EXP_PER_GROUP
TOPK_INNER = 2
LN_EPS = 1e-5

LANES = 128
SUBLANES = 8
TM_IN = 1024
TN_IN = 1024
TM = 256
GLA_CHUNK = 128
DECAY_FACTOR_LIMIT = 80.0
TQ = 128
KEY_BUCKET = 256
SCORE_PAGES = 8
RADIX_BITS = 4
DMA_UNROLL = 8
N_SPARE = 2 * DMA_UNROLL
TK_S = 512
VMEM_LIMIT = 56 * 1024 * 1024
INT_MIN = -(2 ** 31)

C_GQ, C_GK, C_GV, C_GR = 0, 512, 1024, 2048
C_AQ, C_AK, C_AV, C_IQ = 3072, 4096, 4608, 5120
C_GG, C_GD = 6144, 8192
N_MAIN = 10240
W_IN_TILE_OFFSETS = (0, 1024, 2048, 3088, 4112, 5136, 6240, 7264, 8288, 9312)
L_IK, L_GA, L_IW = 0, 64, 80


def _cparams(sem):
    return pltpu.CompilerParams(dimension_semantics=sem, vmem_limit_bytes=VMEM_LIMIT)


def _bdot(a, b):
    return jnp.dot(a.astype(BF16), b.astype(BF16), preferred_element_type=F32)


def _bdot_nt(a, b):
    return lax.dot_general(a.astype(BF16), b.astype(BF16), (((1,), (1,)), ((), ())),
                           preferred_element_type=F32)


def _hilo(x):
    hi = x.astype(BF16)
    return hi, (x - hi.astype(F32)).astype(BF16)


def _dot3(a, w):
    ah, al = _hilo(a)
    wh, wl = _hilo(w)
    d = lambda p, q: jnp.dot(p, q, preferred_element_type=F32)
    return d(ah, wh) + (d(al, wh) + d(ah, wl))


def _dot3_nt(a, wt):
    ah, al = _hilo(a)
    wh, wl = _hilo(wt)
    d = lambda p, q: lax.dot_general(p, q, (((1,), (1,)), ((), ())), preferred_element_type=F32)
    return d(ah, wh) + (d(al, wh) + d(ah, wl))


def _split3(x):
    hi = x.astype(BF16)
    r = x - hi.astype(F32)
    mid = r.astype(BF16)
    lo = (r - mid.astype(F32)).astype(BF16)
    return hi, mid, lo


def _ln(x, g, b):
    mu = jnp.mean(x, axis=-1, keepdims=True)
    xc = x - mu
    var = jnp.mean(xc * xc, axis=-1, keepdims=True)
    return xc * lax.rsqrt(var + LN_EPS) * g + b


def _col_to_row(col, n):
    eye = lax.broadcasted_iota(I32, (n, n), 0) == lax.broadcasted_iota(I32, (n, n), 1)
    return jnp.sum(jnp.where(eye, col, 0.0), axis=0, keepdims=True)


def _row_to_col(row, n):
    eye = lax.broadcasted_iota(I32, (n, n), 0) == lax.broadcasted_iota(I32, (n, n), 1)
    return jnp.sum(jnp.where(eye, row, 0.0), axis=1, keepdims=True)


def _sort_key(x):
    bits = lax.bitcast_convert_type(x, I32)
    return jnp.where(bits < 0, bits ^ jnp.int32(0x7FFFFFFF), bits)


def _ln_in_kernel(x_ref, g_ref, b_ref, o_ref, ob_ref):
    y = _ln(x_ref[...], g_ref[...], b_ref[...])
    o_ref[...] = y
    ob_ref[...] = y.astype(BF16)


def ln_in(x, g, b, tm):
    T, D = x.shape
    return pl.pallas_call(
        _ln_in_kernel,
        grid=(T // tm,),
        in_specs=[pl.BlockSpec((tm, D), lambda i: (i, 0)),
                  pl.BlockSpec((1, D), lambda i: (0, 0)),
                  pl.BlockSpec((1, D), lambda i: (0, 0))],
        out_specs=[pl.BlockSpec((tm, D), lambda i: (i, 0)),
                   pl.BlockSpec((tm, D), lambda i: (i, 0))],
        out_shape=[jax.ShapeDtypeStruct((T, D), F32), jax.ShapeDtypeStruct((T, D), BF16)],
        compiler_params=_cparams(("parallel",)),
        name="ln_in",
    )(x, g.reshape(1, D), b.reshape(1, D))


def _in_proj_kernel(off_ref, x_ref, wt_ref, o_ref, w_s):
    @pl.when(pl.program_id(1) == 0)
    def _():
        w_s[...] = wt_ref[0].T.astype(BF16)

    o_ref[...] = jnp.dot(x_ref[...], w_s[...], preferred_element_type=F32)


def _s_in_proj_kernel(off_ref, x_ref, wt_ref, o_ref):
    o_ref[...] = _dot3_nt(x_ref[...], wt_ref[0])


def _w_in_tile_spec(layer, D):
    return pl.BlockSpec((pl.Element(1), pl.Element(TN_IN), pl.Element(D)),
                        lambda j, *a: (layer, pl.multiple_of(a[-1][j], SUBLANES), 0))


def in_proj_main(xb, w_t, layer, tm):
    T, D = xb.shape
    offs = jnp.asarray(W_IN_TILE_OFFSETS, I32)
    gs = pltpu.PrefetchScalarGridSpec(
        num_scalar_prefetch=1,
        grid=(len(W_IN_TILE_OFFSETS), T // tm),
        in_specs=[pl.BlockSpec((tm, D), lambda j, i, off: (i, 0)), _w_in_tile_spec(layer, D)],
        out_specs=pl.BlockSpec((tm, TN_IN), lambda j, i, off: (i, j)),
        scratch_shapes=[pltpu.VMEM((D, TN_IN), BF16)],
    )
    return pl.pallas_call(
        _in_proj_kernel,
        grid_spec=gs,
        out_shape=jax.ShapeDtypeStruct((T, N_MAIN), F32),
        compiler_params=_cparams(("parallel", "arbitrary")),
        name="in_proj_main",
    )(offs, xb, w_t)


def s_in_proj_main(x, w_t, layer):
    R, D = x.shape
    offs = jnp.asarray(W_IN_TILE_OFFSETS, I32)
    gs = pltpu.PrefetchScalarGridSpec(
        num_scalar_prefetch=1,
        grid=(len(W_IN_TILE_OFFSETS),),
        in_specs=[pl.BlockSpec((R, D), lambda j, off: (0, 0)), _w_in_tile_spec(layer, D)],
        out_specs=pl.BlockSpec((R, TN_IN), lambda j, off: (0, j)),
    )
    return pl.pallas_call(
        _s_in_proj_kernel,
        grid_spec=gs,
        out_shape=jax.ShapeDtypeStruct((R, N_MAIN), F32),
        compiler_params=_cparams(("parallel",)),
        name="s_in_proj_main",
    )(offs, x, w_t)


def _mm_kernel(x_ref, w_ref, o_ref):
    o_ref[...] = jnp.dot(x_ref[...], w_ref[...], preferred_element_type=F32)


def _s_mm_kernel(x_ref, w_ref, o_ref):
    o_ref[...] = _dot3(x_ref[...], w_ref[...])


def in_proj_small(x, w, layer, tm, precise):
    T, D = x.shape
    N = w.shape[-1]
    return pl.pallas_call(
        _s_mm_kernel if precise else _mm_kernel,
        grid=(T // tm,),
        in_specs=[pl.BlockSpec((tm, D), lambda i: (i, 0)),
                  pl.BlockSpec((None, D, N), lambda i: (layer, 0, 0))],
        out_specs=pl.BlockSpec((tm, N), lambda i: (i, 0)),
        out_shape=jax.ShapeDtypeStruct((T, N), F32),
        compiler_params=_cparams(("parallel",)),
        name="in_proj_small",
    )(x, w)


def _rope_tile(x, c, s1, s2, shift):
    w = x.shape[-1]
    xm = pltpu.roll(x, w - shift, axis=1)
    xp = pltpu.roll(x, shift, axis=1)
    return x * c + xm * s1 + xp * s2


def _rope_kernel(aq_ref, ak_ref, av_ref, iq_ref, hs_ref, t128_ref, t64_ref,
                 ka_ref, va_ref, ki_ref, qa_ref, kab_ref, vab_ref, qib_ref, kie_ref, kio_ref):
    c1, s1a, s1b = t128_ref[0], t128_ref[1], t128_ref[2]
    c2, s2a, s2b = t64_ref[0], t64_ref[1], t64_ref[2]
    for hd in range(ATT_HEADS):
        sl = slice(hd * HEAD_DIM, (hd + 1) * HEAD_DIM)
        qa_ref[:, sl] = _rope_tile(aq_ref[:, sl], c1, s1a, s1b, HEAD_DIM // 8).astype(qa_ref.dtype)
    for hd in range(KV_HEADS):
        sl = slice(hd * HEAD_DIM, (hd + 1) * HEAD_DIM)
        kr = _rope_tile(ak_ref[:, sl], c1, s1a, s1b, HEAD_DIM // 8)
        ka_ref[:, sl] = kr
        kab_ref[:, sl] = kr.astype(BF16)
    v = av_ref[...]
    va_ref[...] = v
    if vab_ref.shape[0] == v.shape[0]:
        vab_ref[...] = v.astype(BF16)
    else:
        vab_ref[...] = v.T.astype(BF16)
    for pr in range(IDX_HEADS * IDX_DIM // LANES):
        sl = slice(pr * LANES, (pr + 1) * LANES)
        qr = _rope_tile(iq_ref[:, sl], c2, s2a, s2b, IDX_DIM // 8) * (IDX_DIM ** -0.5)
        qib_ref[:, sl] = qr.astype(qib_ref.dtype)
    kir = _rope_tile(hs_ref[...], c2, s2a, s2b, IDX_DIM // 8)
    ki_ref[...] = kir[:, :IDX_DIM]
    lane = lax.broadcasted_iota(I32, kir.shape, 1)
    ke = jnp.where(lane < IDX_DIM, kir, 0.0)
    kie_ref[...] = ke.astype(BF16)
    kio_ref[...] = pltpu.roll(ke, IDX_DIM, axis=1).astype(BF16)


def rope_stage(h, hs, t128, t64, tm, q_dtype, v_transposed):
    T = h.shape[0]
    aw, kw, iw = ATT_HEADS * HEAD_DIM, KV_HEADS * HEAD_DIM, IDX_HEADS * IDX_DIM
    row = lambda w, c: pl.BlockSpec((tm, w), lambda i: (i, c))
    tab = pl.BlockSpec((3, tm, LANES), lambda i: (0, i, 0))
    vb_spec = pl.BlockSpec((kw, tm), lambda i: (0, i)) if v_transposed else row(kw, 0)
    vb_shape = (kw, T) if v_transposed else (T, kw)
    return pl.pallas_call(
        _rope_kernel,
        grid=(T // tm,),
        in_specs=[row(aw, C_AQ // aw), row(kw, C_AK // kw), row(kw, C_AV // kw), row(iw, C_IQ // iw),
                  row(LANES, 0), tab, tab],
        out_specs=[row(kw, 0), row(kw, 0), row(IDX_DIM, 0), row(aw, 0), row(kw, 0), vb_spec,
                   row(iw, 0), row(LANES, 0), row(LANES, 0)],
        out_shape=[jax.ShapeDtypeStruct((T, kw), F32), jax.ShapeDtypeStruct((T, kw), F32),
                   jax.ShapeDtypeStruct((T, IDX_DIM), F32), jax.ShapeDtypeStruct((T, aw), q_dtype),
                   jax.ShapeDtypeStruct((T, kw), BF16), jax.ShapeDtypeStruct(vb_shape, BF16),
                   jax.ShapeDtypeStruct((T, iw), q_dtype), jax.ShapeDtypeStruct((T, LANES), BF16),
                   jax.ShapeDtypeStruct((T, LANES), BF16)],
        compiler_params=_cparams(("parallel",)),
        name="rope_stage",
    )(h, h, h, h, hs, t128, t64)


def _log_decay(hs, wdp_ref, bd_ref, precise=False):
    z = (_dot3 if precise else _bdot)(hs, wdp_ref[...]) + bd_ref[...]
    return (jnp.minimum(z, 0.0) - jnp.log1p(jnp.exp(-jnp.abs(z)))) * (1.0 / GLA_TAU)


def _gla_out(o, gn, gr):
    on = o * lax.rsqrt(jnp.mean(o * o, axis=-1, keepdims=True) + LN_EPS) * gn
    return on * (gr * jax.nn.sigmoid(gr))


def _gla_prompt_kernel(hq_ref, hk_ref, hv_ref, hr_ref, hs_ref, wdp_ref, bd_ref, gn_ref,
                       og_ref, sfin_ref, s_ref, att_s, b_s, q_s):
    c = pl.program_id(1)
    C = GLA_CHUNK

    @pl.when(c == 0)
    def _():
        s_ref[...] = jnp.zeros_like(s_ref)

    la_all = _log_decay(hs_ref[...], wdp_ref, bd_ref)
    r_i = lax.broadcasted_iota(I32, (C, C), 0)
    c_i = lax.broadcasted_iota(I32, (C, C), 1)
    causal = r_i >= c_i
    tri = jnp.where(causal, 1.0, 0.0).astype(BF16)
    s_idx = lax.broadcasted_iota(I32, (C, 1), 0)
    gn = gn_ref[...]
    for hd in range(GLA_HEADS):
        ks = slice(hd * GLA_DK, (hd + 1) * GLA_DK)
        vs = slice(hd * GLA_DV, (hd + 1) * GLA_DV)
        la = la_all[:, ks]
        hi, mid, lo = _split3(la)
        b = (jnp.dot(tri, hi, preferred_element_type=F32)
             + jnp.dot(tri, mid, preferred_element_type=F32)
             + jnp.dot(tri, lo, preferred_element_type=F32))
        eb = jnp.exp(b)
        qe = hq_ref[:, ks] * (GLA_DK ** -0.5) * eb
        k = hk_ref[:, ks]
        v = hv_ref[:, vs]
        s_old = s_ref[hd]
        b_last = b[C - 1:C, :]
        factor_ok = jnp.max(-b_last) <= DECAY_FACTOR_LIMIT

        @pl.when(factor_ok)
        def _():
            att_s[...] = jnp.where(causal, _bdot_nt(qe, k * jnp.exp(-b)), 0.0)

        @pl.when(jnp.logical_not(factor_ok))
        def _():
            b_s[...] = b
            q_s[...] = hq_ref[:, ks] * (GLA_DK ** -0.5)

            def row(t, carry):
                qt = q_s[pl.ds(t, 1), :]
                d = jnp.exp(jnp.minimum(b_s[pl.ds(t, 1), :] - b, 0.0))
                col = jnp.sum(k * d * qt, axis=1, keepdims=True)
                col = jnp.where(s_idx <= t, col, 0.0)
                att_s[pl.ds(t, 1), :] = _col_to_row(jnp.broadcast_to(col, (C, C)), C)
                return carry

            lax.fori_loop(0, C, row, 0)

        o = _bdot(att_s[...], v) + _bdot(qe, s_old)
        og_ref[:, vs] = _gla_out(o, gn, hr_ref[:, vs]).astype(BF16)
        eb_last = eb[C - 1:C, :]
        kdec = k * jnp.exp(b_last - b)
        upd = lax.dot_general(kdec.astype(BF16), v.astype(BF16), (((0,), (0,)), ((), ())),
                              preferred_element_type=F32)
        s_new = s_old * _row_to_col(jnp.broadcast_to(eb_last, (GLA_DK, GLA_DK)), GLA_DK) + upd
        s_ref[hd] = s_new

        @pl.when(c == pl.num_programs(1) - 1)
        def _():
            sfin_ref[0, hd] = s_new


def gla_prompt(h, hs, wdp, bd, gn, B, S):
    nC = S // GLA_CHUNK
    kw, vw = GLA_HEADS * GLA_DK, GLA_HEADS * GLA_DV
    row = lambda w, col: pl.BlockSpec((GLA_CHUNK, w), lambda b, c: (b * nC + c, col))
    cst = lambda shp: pl.BlockSpec(shp, lambda b, c: (0,) * len(shp))
    return pl.pallas_call(
        _gla_prompt_kernel,
        grid=(B, nC),
        in_specs=[row(kw, C_GQ // kw), row(kw, C_GK // kw), row(vw, C_GV // vw), row(vw, C_GR // vw),
                  row(LANES, 0), cst((LANES, kw)), cst((1, kw)), cst((1, GLA_DV))],
        out_specs=[row(vw, 0),
                   pl.BlockSpec((1, GLA_HEADS, GLA_DK, GLA_DV), lambda b, c: (b, 0, 0, 0))],
        out_shape=[jax.ShapeDtypeStruct((B * S, vw), BF16),
                   jax.ShapeDtypeStruct((B, GLA_HEADS, GLA_DK, GLA_DV), F32)],
        scratch_shapes=[pltpu.VMEM((GLA_HEADS, GLA_DK, GLA_DV), F32),
                        pltpu.VMEM((GLA_CHUNK, GLA_CHUNK), F32), pltpu.VMEM((GLA_CHUNK, GLA_DK), F32),
                        pltpu.VMEM((GLA_CHUNK, GLA_DK), F32)],
        compiler_params=_cparams(("arbitrary", "arbitrary")),
        name="gla_prompt",
    )(h, h, h, h, hs, wdp, bd, gn)


def _gla_sample_kernel(hq_ref, hk_ref, hv_ref, hr_ref, hs_ref, wdp_ref, bd_ref, gn_ref, s0_ref,
                       og_ref, s1_ref):
    b = pl.program_id(0)
    la_all = _log_decay(hs_ref[...], wdp_ref, bd_ref, precise=True)
    gn = gn_ref[...]
    rowsel = lax.broadcasted_iota(I32, (hq_ref.shape[0], 1), 0) == b

    def pick(x):
        return jnp.sum(jnp.where(rowsel, x, 0.0), axis=0, keepdims=True)

    for hd in range(GLA_HEADS):
        ks = slice(hd * GLA_DK, (hd + 1) * GLA_DK)
        vs = slice(hd * GLA_DV, (hd + 1) * GLA_DV)
        a_row = jnp.exp(pick(la_all[:, ks]))
        q_row = pick(hq_ref[:, ks]) * (GLA_DK ** -0.5)
        k_row = pick(hk_ref[:, ks])
        v_row = pick(hv_ref[:, vs])
        bc = lambda r: _row_to_col(jnp.broadcast_to(r, (GLA_DK, GLA_DK)), GLA_DK)
        s_new = s0_ref[0, hd] * bc(a_row) + bc(k_row) * v_row
        s1_ref[0, hd] = s_new
        o = jnp.sum(bc(q_row) * s_new, axis=0, keepdims=True)
        og_ref[pl.ds(b, 1), vs] = _gla_out(o, gn, pick(hr_ref[:, vs]))


def gla_sample(h, hs, wdp, bd, gn, state, layer):
    DB = h.shape[0]
    kw, vw = GLA_HEADS * GLA_DK, GLA_HEADS * GLA_DV
    row = lambda w, col: pl.BlockSpec((DB, w), lambda b: (0, col))
    cst = lambda shp: pl.BlockSpec(shp, lambda b: (0,) * len(shp))
    return pl.pallas_call(
        _gla_sample_kernel,
        grid=(DB,),
        in_specs=[row(kw, C_GQ // kw), row(kw, C_GK // kw), row(vw, C_GV // vw), row(vw, C_GR // vw),
                  row(LANES, 0), cst((LANES, kw)), cst((1, kw)), cst((1, GLA_DV)),
                  pl.BlockSpec((None, 1, GLA_HEADS, GLA_DK, GLA_DV), lambda b: (layer, b, 0, 0, 0))],
        out_specs=[pl.BlockSpec((DB, vw), lambda b: (0, 0)),
                   pl.BlockSpec((1, GLA_HEADS, GLA_DK, GLA_DV), lambda b: (b, 0, 0, 0))],
        out_shape=[jax.ShapeDtypeStruct((DB, vw), F32),
                   jax.ShapeDtypeStruct((DB, GLA_HEADS, GLA_DK, GLA_DV), F32)],
        compiler_params=_cparams(("arbitrary",)),
        name="gla_sample",
    )(h, h, h, h, hs, wdp, bd, gn, state)


def _colreduce(x, red, pair, groups=8):
    n, C = x.shape
    x3 = x.reshape(n // SUBLANES, SUBLANES, C)
    rows = x3.shape[0] // groups
    parts = [red(x3[g * rows:(g + 1) * rows], axis=0) for g in range(groups)]
    while len(parts) > 1:
        parts = [pair(parts[j], parts[j + 1]) for j in range(0, len(parts), 2)]
    return red(parts[0], axis=0, keepdims=True)


def _colsum(x):
    return _colreduce(x, jnp.sum, jnp.add)


def _colmax(x):
    return _colreduce(x, jnp.max, jnp.maximum)


def _kth_threshold_t(key, topk):
    C = key.shape[1]

    def body(i, t):
        cand = t + jnp.left_shift(jnp.int32(1), 31 - i)
        cnt = _colsum(jnp.where(key >= cand, 1.0, 0.0))
        return jnp.where(cnt >= float(topk), cand, t)

    return lax.fori_loop(0, 32, body, jnp.full((1, C), INT_MIN, I32))


def _dsa_prompt_body(qb, nk, qa_ref, qi_ref, hs_ref, ka_ref, vt_ref, kie_ref, kio_ref, o_ref, sel_s, topk):
    nt = (((1,), (1,)), ((), ()))
    hs_t = hs_ref[...].T
    acc = jnp.zeros((nk, TQ), F32)
    for hd in range(IDX_HEADS):
        pr = hd // 2
        q = qi_ref[:, pr * LANES:(pr + 1) * LANES]
        kk = kie_ref[:nk, :] if hd % 2 == 0 else kio_ref[:nk, :]
        s = lax.dot_general(kk, q, nt, preferred_element_type=F32)
        w = hs_t[L_IW + hd:L_IW + hd + 1, :] * (IDX_HEADS ** -0.5)
        acc = acc + w * jnp.maximum(s, 0.0)
    kpos = lax.broadcasted_iota(I32, (nk, TQ), 0)
    qpos = qb * TQ + lax.broadcasted_iota(I32, (nk, TQ), 1)
    vis = kpos <= qpos
    if nk > topk:
        key = jnp.where(vis, _sort_key(acc), INT_MIN)
        thr = _kth_threshold_t(key, topk)
        ge = jnp.where(jnp.logical_and(key >= thr, vis), 1.0, 0.0)
        sel_s[:nk, :] = ge
        n_ge = _colsum(ge)

        @pl.when(jnp.max(n_ge) > float(topk))
        def _():
            gt = jnp.where(jnp.logical_and(key > thr, vis), 1.0, 0.0)
            eq = ge - gt
            need = float(topk) - _colsum(gt)
            r_b = lax.broadcasted_iota(I32, (LANES, LANES), 0)
            c_b = lax.broadcasted_iota(I32, (LANES, LANES), 1)
            below = jnp.where(c_b < r_b, 1.0, 0.0).astype(BF16)
            seen = jnp.zeros((1, TQ), F32)
            for kb in range(nk // LANES):
                blk = slice(kb * LANES, (kb + 1) * LANES)
                e = eq[blk]
                rank = jnp.dot(below, e.astype(BF16), preferred_element_type=F32) + seen
                sel_s[blk, :] = gt[blk] + jnp.where(rank < need, e, 0.0)
                seen = seen + jnp.sum(e, axis=0, keepdims=True)

        sel = sel_s[:nk, :] > 0.5
    else:
        sel = vis
    G = ATT_HEADS // KV_HEADS
    for n in range(KV_HEADS):
        ksl = slice(n * HEAD_DIM, (n + 1) * HEAD_DIM)
        kn = ka_ref[:nk, ksl]
        vt = vt_ref[ksl, :nk]
        for g in range(G):
            hsl = slice((n * G + g) * HEAD_DIM, (n * G + g + 1) * HEAD_DIM)
            s = lax.dot_general(kn, qa_ref[:, hsl], nt, preferred_element_type=F32) * (HEAD_DIM ** -0.5)
            s = jnp.where(sel, s, -jnp.inf)
            m = _colmax(s)
            p = jnp.exp(s - m)
            l = _colsum(p)
            o_t =jnp.dot(vt, p.astype(BF16), preferred_element_type=F32) / l
            o_ref[:, hsl] = o_t.T.astype(BF16)


def _dsa_prompt_kernel(qa_ref, qi_ref, hs_ref, ka_ref, va_ref, kie_ref, kio_ref, o_ref, sel_s, *, topk, bucket):
    qb = pl.program_id(1)
    S = ka_ref.shape[0]
    for bi in range(S // bucket):
        @pl.when(qb // (bucket // TQ) == bi)
        def _():
            _dsa_prompt_body(qb, (bi + 1) * bucket, qa_ref, qi_ref, hs_ref, ka_ref, va_ref, kie_ref, kio_ref,
                             o_ref, sel_s, topk)


def dsa_prompt(qab, qib, hs, kab, vab_t, kie, kio, B, S):
    nq = S // TQ
    aw, kw, iw = ATT_HEADS * HEAD_DIM, KV_HEADS * HEAD_DIM, IDX_HEADS * IDX_DIM
    topk = min(TOPK_MAX, S // 4)
    bucket = min(KEY_BUCKET, S)
    assert S % bucket == 0 and bucket % TQ == 0
    qrow = lambda w: pl.BlockSpec((TQ, w), lambda b, q: (b * nq + q, 0))
    seq = lambda w: pl.BlockSpec((S, w), lambda b, q: (b, 0))
    return pl.pallas_call(
        functools.partial(_dsa_prompt_kernel, topk=topk, bucket=bucket),
        grid=(B, nq),
        in_specs=[qrow(aw), qrow(iw), qrow(LANES), seq(kw), pl.BlockSpec((kw, S), lambda b, q: (0, b)),
                  seq(LANES), seq(LANES)],
        out_specs=qrow(aw),
        out_shape=jax.ShapeDtypeStruct((B * S, aw), BF16),
        scratch_shapes=[pltpu.VMEM((S, TQ), F32)],
        compiler_params=_cparams(("parallel", "parallel")),
        name="dsa_prompt",
    )(qab, qib, hs, kab, vab_t, kie, kio)


def _dsa_sample_kernel(pt_ref, qi_ref, w_ref, kin_ref, qa_ref, kn_ref, vn_ref,
                       ckidx_hbm, ck_hbm, cv_hbm, o_ref,
                       kbuf, sc_ref, idxv_ref, idxs_ref, kg_ref, vg_ref, sem_i, sem_x, sem_k, sem_v,
                       *, layer, n_pages, page, topk):
    b = pl.program_id(0)
    NP, PG, K = n_pages, page, topk

    def icopy(p):
        return pltpu.make_async_copy(ckidx_hbm.at[layer, pt_ref[b, p]], kbuf.at[p], sem_i)

    lax.fori_loop(0, NP, lambda p, c: (icopy(p).start(), c)[1], 0)
    lax.fori_loop(0, NP, lambda p, c: (icopy(p).wait(), c)[1], 0)

    q = qi_ref[0]
    wcol = w_ref[0]

    def score_chunk(c, carry):
        pages = kbuf[pl.ds(c * SCORE_PAGES, SCORE_PAGES)]
        kc = jnp.concatenate([pages[j] for j in range(SCORE_PAGES)], axis=1)
        s = _dot3(q, kc)
        r = jnp.sum(wcol * jnp.maximum(s, 0.0), axis=0, keepdims=True)
        for j in range(SCORE_PAGES):
            sc_ref[pl.ds(c * SCORE_PAGES + j, 1), :] = r[:, j * PG:(j + 1) * PG]
        return carry

    lax.fori_loop(0, NP // SCORE_PAGES, score_chunk, 0)
    s_new = jnp.sum(q * kin_ref[0], axis=1, keepdims=True)
    s_new = jnp.sum(wcol * jnp.maximum(s_new, 0.0), axis=0, keepdims=True)

    key = _sort_key(sc_ref[...])
    key_new = _sort_key(s_new)

    def body(i, t):
        shift = 32 - RADIX_BITS * (i + 1)
        digit = jnp.zeros((1, 1), I32)
        for d in range(1, 2 ** RADIX_BITS):
            cand = t + jnp.left_shift(jnp.int32(d), shift)
            cnt = jnp.sum(jnp.sum(jnp.where(key >= cand, 1.0, 0.0), axis=0, keepdims=True), axis=1, keepdims=True)
            cnt = cnt + jnp.where(key_new >= cand, 1.0, 0.0)
            digit = digit + jnp.where(cnt >= float(K), 1, 0)
        return t + jnp.left_shift(digit, shift)

    thr = lax.fori_loop(0, 32 // RADIX_BITS, body, jnp.full((1, 1), INT_MIN, I32))
    gt = jnp.where(key > thr, 1.0, 0.0)
    eq = jnp.where(key == thr, 1.0, 0.0)
    n_gt = (jnp.sum(jnp.sum(gt, axis=1, keepdims=True), axis=0, keepdims=True)
            + jnp.where(key_new > thr, 1.0, 0.0))
    need = float(K) - n_gt

    r_p = lax.broadcasted_iota(I32, (PG, PG), 0)
    c_p = lax.broadcasted_iota(I32, (PG, PG), 1)
    ut = jnp.where(r_p <= c_p, 1.0, 0.0).astype(BF16)
    r_n = lax.broadcasted_iota(I32, (NP, NP), 0)
    c_n = lax.broadcasted_iota(I32, (NP, NP), 1)
    slt = jnp.where(c_n < r_n, 1.0, 0.0).astype(BF16)

    def prefix(m):
        cs = jnp.dot(m.astype(BF16), ut, preferred_element_type=F32)
        tot = cs[:, PG - 1:PG]
        off = jnp.dot(slt, jnp.broadcast_to(tot, (NP, PG)).astype(BF16), preferred_element_type=F32)[:, 0:1]
        return cs, off, tot

    cs_e, off_e, tot_e = prefix(eq)
    tie_rank = off_e + cs_e - eq
    sel = jnp.maximum(gt, jnp.where(tie_rank < need, eq, 0.0))
    n_eq_past = jnp.sum(tot_e, axis=0, keepdims=True)
    sel_new = jnp.logical_or(key_new > thr, jnp.logical_and(key_new == thr, n_eq_past < need))

    cs, off, tot = prefix(sel)
    n_past_sel = jnp.sum(tot, axis=0, keepdims=True)
    ci_row = _col_to_row(jnp.broadcast_to(off + tot, (NP, NP)), NP)
    off_row = _col_to_row(jnp.broadcast_to(off, (NP, NP)), NP)
    jcol = lax.broadcasted_iota(I32, (K, 1), 0).astype(F32)
    page_of = jnp.sum(jnp.where(ci_row <= jcol, 1.0, 0.0), axis=1, keepdims=True)
    lane_p = lax.broadcasted_iota(I32, (K, NP), 1).astype(F32)
    onehot = jnp.where(lane_p == page_of, 1.0, 0.0)
    off_j = jnp.sum(onehot * off_row, axis=1, keepdims=True)
    lr = jcol - off_j
    cs_row = jnp.dot(onehot.astype(BF16), cs.astype(BF16), preferred_element_type=F32)
    sel_row = jnp.dot(onehot.astype(BF16), sel.astype(BF16), preferred_element_type=F32)
    lane_o = lax.broadcasted_iota(I32, (K, PG), 1).astype(F32)
    hit = jnp.logical_and(cs_row - 1.0 == lr, sel_row > 0.5)
    off_of = jnp.sum(jnp.where(hit, lane_o, 0.0), axis=1, keepdims=True)
    valid_j = jcol < n_past_sel
    page_of = jnp.where(valid_j, page_of, 0.0)
    off_of = jnp.where(valid_j, off_of, 0.0)
    idxv_ref[...] = jnp.zeros_like(idxv_ref)
    nh = K // LANES
    for hh in range(nh):
        pg_r = _col_to_row(jnp.broadcast_to(page_of[hh * LANES:(hh + 1) * LANES], (LANES, LANES)), LANES)
        of_r = _col_to_row(jnp.broadcast_to(off_of[hh * LANES:(hh + 1) * LANES], (LANES, LANES)), LANES)
        idxv_ref[hh:hh + 1, :] = pg_r.astype(I32)
        idxv_ref[nh + hh:nh + hh + 1, :] = of_r.astype(I32)
    xcp = pltpu.make_async_copy(idxv_ref, idxs_ref, sem_x)
    xcp.start()
    xcp.wait()

    def kcopy(j):
        pg = idxs_ref[j // LANES, j % LANES]
        of = idxs_ref[nh + j // LANES, j % LANES]
        phys = pt_ref[b, pg]
        return (pltpu.make_async_copy(ck_hbm.at[layer, phys, of], kg_ref.at[j], sem_k),
                pltpu.make_async_copy(cv_hbm.at[layer, phys, of], vg_ref.at[j], sem_v))

    def start_j(c, carry):
        for u in range(DMA_UNROLL):
            ck, cv = kcopy(c * DMA_UNROLL + u)
            ck.start()
            cv.start()
        return carry

    def wait_j(c, carry):
        for u in range(DMA_UNROLL):
            ck, cv = kcopy(c * DMA_UNROLL + u)
            ck.wait()
            cv.wait()
        return carry

    lax.fori_loop(0, K // DMA_UNROLL, start_j, 0)
    lax.fori_loop(0, K // DMA_UNROLL, wait_j, 0)

    is_new = jnp.logical_and(jnp.logical_not(valid_j), sel_new)
    G = ATT_HEADS // KV_HEADS
    for n in range(KV_HEADS):
        kn = jnp.where(is_new, kn_ref[0, n:n + 1, :], kg_ref[:, n, :])
        vn = jnp.where(is_new, vn_ref[0, n:n + 1, :], vg_ref[:, n, :])
        qn = qa_ref[0, n * G:(n + 1) * G, :]
        s = _dot3_nt(qn, kn) * (HEAD_DIM ** -0.5)
        m = jnp.max(s, axis=1, keepdims=True)
        p = jnp.exp(s - m)
        l = jnp.sum(p, axis=1, keepdims=True)
        o_ref[0, n * G:(n + 1) * G, :] = _dot3(p, vn) / l


def dsa_sample(page_table, qi_s, w_s, kin_s, qa_s, kn_s, vn_s, cache_kidx_t, cache_k, cache_v, layer):
    DB, n_pages = page_table.shape
    page = cache_k.shape[2]
    L = n_pages * page + 1
    topk = min(TOPK_MAX, L // 4)
    assert topk % LANES == 0 and topk <= n_pages * page and n_pages % SCORE_PAGES == 0
    blk = lambda shp: pl.BlockSpec((1,) + shp, lambda b, pt: (b,) + (0,) * len(shp))
    any_spec = pl.BlockSpec(memory_space=pl.ANY)
    gs = pltpu.PrefetchScalarGridSpec(
        num_scalar_prefetch=1,
        grid=(DB,),
        in_specs=[blk((IDX_HEADS, IDX_DIM)), blk((IDX_HEADS, 1)), blk((1, IDX_DIM)),
                  blk((ATT_HEADS, HEAD_DIM)), blk((KV_HEADS, HEAD_DIM)), blk((KV_HEADS, HEAD_DIM)),
                  any_spec, any_spec, any_spec],
        out_specs=blk((ATT_HEADS, HEAD_DIM)),
        scratch_shapes=[pltpu.VMEM((n_pages, IDX_DIM, page), F32),
                        pltpu.VMEM((n_pages, page), F32),
                        pltpu.VMEM((SUBLANES, LANES), I32),
                        pltpu.SMEM((SUBLANES, LANES), I32),
                        pltpu.VMEM((topk, KV_HEADS, HEAD_DIM), F32),
                        pltpu.VMEM((topk, KV_HEADS, HEAD_DIM), F32),
                        pltpu.SemaphoreType.DMA, pltpu.SemaphoreType.DMA,
                        pltpu.SemaphoreType.DMA, pltpu.SemaphoreType.DMA],
    )
    return pl.pallas_call(
        functools.partial(_dsa_sample_kernel, layer=layer, n_pages=n_pages, page=page, topk=topk),
        grid_spec=gs,
        out_shape=jax.ShapeDtypeStruct((DB, ATT_HEADS, HEAD_DIM), F32),
        compiler_params=_cparams(("arbitrary",)),
        name="dsa_sample",
    )(page_table, qi_s, w_s, kin_s, qa_s, kn_s, vn_s, cache_kidx_t, cache_k, cache_v)


def _route(lg, ri_ref, rw_ref):
    lane = lax.broadcasted_iota(I32, lg.shape, 1)
    big = jnp.int32(1 << 20)
    neg = -jnp.inf
    l_g = jnp.where(lane < N_GROUPS, lg, neg)
    m_g = jnp.max(l_g, axis=1, keepdims=True)
    g_top = jnp.min(jnp.where(l_g == m_g, lane, big), axis=1, keepdims=True)
    pg_top = 1.0 / jnp.sum(jnp.exp(l_g - m_g), axis=1, keepdims=True)
    e_lane = lane - N_GROUPS
    in_grp = jnp.logical_and(e_lane >= g_top * EXP_PER_GROUP, e_lane < (g_top + 1) * EXP_PER_GROUP)
    l_e = jnp.where(in_grp, lg, neg)
    m1 = jnp.max(l_e, axis=1, keepdims=True)
    i1 = jnp.min(jnp.where(l_e == m1, e_lane, big), axis=1, keepdims=True)
    l_e2 = jnp.where(e_lane == i1, neg, l_e)
    m2 = jnp.max(l_e2, axis=1, keepdims=True)
    i2 = jnp.min(jnp.where(l_e2 == m2, e_lane, big), axis=1, keepdims=True)
    e2 = jnp.exp(m2 - m1)
    w1 = pg_top / (1.0 + e2)
    w2 = pg_top * e2 / (1.0 + e2)
    ri_ref[...] = jnp.where(lane == 0, i1, jnp.where(lane == 1, i2, 0))
    rw_ref[...] = jnp.where(lane == 0, w1, jnp.where(lane == 1, w2, 0.0))


def _mix_kernel(og_ref, at_ref, gg_ref, gd_ref, x_ref, wbg_ref, wbd_ref, wo_ref, g_ref, b_ref,
                wrh_ref, wrl_ref, br_ref, x1_ref, ri_ref, rw_ref, *, alpha):
    bg = jnp.dot(og_ref[...], wbg_ref[...], preferred_element_type=F32)
    bd = jnp.dot(at_ref[...], wbd_ref[...], preferred_element_type=F32)
    merged = jax.nn.sigmoid(gg_ref[...]) * bg + jax.nn.sigmoid(gd_ref[...]) * bd
    mix = jnp.dot(merged.astype(BF16), wo_ref[...], preferred_element_type=F32)
    x1 = _ln(alpha * x_ref[...] + mix, g_ref[...], b_ref[...])
    x1_ref[...] = x1
    xh, xl = _hilo(x1)
    d = lambda p, q: jnp.dot(p, q, preferred_element_type=F32)
    lg = d(xh, wrh_ref[...]) + (d(xh, wrl_ref[...]) + d(xl, wrh_ref[...])) + br_ref[...]
    _route(lg, ri_ref, rw_ref)


def mix_stage(og, att, h, x, wbg, wbd, wo, g, b, wrh, wrl, br, layer, alpha):
    T, D = x.shape
    gw = GLA_HEADS * GLA_DV
    aw = ATT_HEADS * HEAD_DIM
    row = lambda w, c: pl.BlockSpec((TM, w), lambda i: (i, c))
    wsp = lambda r, c: pl.BlockSpec((None, r, c), lambda i: (layer, 0, 0))
    return pl.pallas_call(
        functools.partial(_mix_kernel, alpha=alpha),
        grid=(T // TM,),
        in_specs=[row(gw, 0), row(aw, 0), row(D, C_GG // D), row(D, C_GD // D), row(D, 0),
                  wsp(gw, D), wsp(aw, D), wsp(D, D), wsp(1, D), wsp(1, D),
                  wsp(D, LANES), wsp(D, LANES), wsp(1, LANES)],
        out_specs=[row(D, 0), row(LANES, 0), row(LANES, 0)],
        out_shape=[jax.ShapeDtypeStruct((T, D), F32), jax.ShapeDtypeStruct((T, LANES), I32),
                   jax.ShapeDtypeStruct((T, LANES), F32)],
        compiler_params=_cparams(("parallel",)),
        name="mix_stage",
    )(og, att, h, h, x, wbg, wbd, wo, g, b, wrh, wrl, br)


def _s_merge_kernel(og_ref, at_ref, gg_ref, gd_ref, wbg_ref, wbd_ref, o_ref):
    bg = _dot3(og_ref[...], wbg_ref[...])
    bd = _dot3(at_ref[...], wbd_ref[...])
    o_ref[...] = jax.nn.sigmoid(gg_ref[...]) * bg + jax.nn.sigmoid(gd_ref[...]) * bd


def s_merge(og, att, h, wbg, wbd, layer):
    R = og.shape[0]
    D = wbg.shape[-1]
    gw, aw = og.shape[1], att.shape[1]
    return pl.pallas_call(
        _s_merge_kernel,
        grid=(D // TK_S,),
        in_specs=[pl.BlockSpec((R, gw), lambda j: (0, 0)), pl.BlockSpec((R, aw), lambda j: (0, 0)),
                  pl.BlockSpec((R, TK_S), lambda j: (0, C_GG // TK_S + j)),
                  pl.BlockSpec((R, TK_S), lambda j: (0, C_GD // TK_S + j)),
                  pl.BlockSpec((None, gw, TK_S), lambda j: (layer, 0, j)),
                  pl.BlockSpec((None, aw, TK_S), lambda j: (layer, 0, j))],
        out_specs=pl.BlockSpec((R, TK_S), lambda j: (0, j)),
        out_shape=jax.ShapeDtypeStruct((R, D), F32),
        compiler_params=_cparams(("parallel",)),
        name="s_merge",
    )(og, att, h, h, wbg, wbd)


def _s_outproj_kernel(m_ref, x_ref, wo_ref, g_ref, b_ref, wr_ref, br_ref, x1_ref, ri_ref, rw_ref, acc,
                      *, alpha):
    k = pl.program_id(0)

    @pl.when(k == 0)
    def _():
        acc[...] = jnp.zeros_like(acc)

    acc[...] += _dot3(m_ref[...], wo_ref[...])

    @pl.when(k == pl.num_programs(0) - 1)
    def _():
        x1 = _ln(alpha * x_ref[...] + acc[...], g_ref[...], b_ref[...])
        x1_ref[...] = x1
        _route(_dot3(x1, wr_ref[...]) + br_ref[...], ri_ref, rw_ref)


def s_outproj(merged, x, wo, g, b, w_rt, br, layer, alpha):
    R, D = x.shape
    cst = lambda r, c: pl.BlockSpec((None, r, c), lambda k: (layer, 0, 0))
    full = lambda w: pl.BlockSpec((R, w), lambda k: (0, 0))
    return pl.pallas_call(
        functools.partial(_s_outproj_kernel, alpha=alpha),
        grid=(D // TK_S,),
        in_specs=[pl.BlockSpec((R, TK_S), lambda k: (0, k)), full(D),
                  pl.BlockSpec((None, TK_S, D), lambda k: (layer, k, 0)),
                  cst(1, D), cst(1, D), cst(D, LANES), cst(1, LANES)],
        out_specs=[full(D), full(LANES), full(LANES)],
        out_shape=[jax.ShapeDtypeStruct((R, D), F32), jax.ShapeDtypeStruct((R, LANES), I32),
                   jax.ShapeDtypeStruct((R, LANES), F32)],
        scratch_shapes=[pltpu.VMEM((R, D), F32)],
        compiler_params=_cparams(("arbitrary",)),
        name="s_outproj",
    )(merged, x, wo, g, b, w_rt, br)


def _moe_kernel(te_ref, nv_ref, tok_ref, tok_nx_ref, pair_ref, pair_pv_ref, x_hbm, wg_ref, wu_ref, wd_ref, y_hbm,
                xbuf, ybuf, gsem, ssem, *, precise):
    i = pl.program_id(0)
    n = pl.num_programs(0)
    cur = i % 2
    nv = nv_ref[i]
    nv_next = jnp.where(i + 1 < n, nv_ref[jnp.minimum(i + 1, n - 1)], 0)
    nv_prev = jnp.where(i > 0, nv_ref[jnp.maximum(i - 1, 0)], 0)

    def chunked(count, fn):
        def body(c, carry):
            for u in range(DMA_UNROLL):
                fn(c * DMA_UNROLL + u)
            return carry
        lax.fori_loop(0, (count + DMA_UNROLL - 1) // DMA_UNROLL, body, 0)

    def gcopy(ids_ref, buf, r):
        return pltpu.make_async_copy(x_hbm.at[pl.ds(ids_ref[0, 0, r], 1)], xbuf.at[buf, pl.ds(r, 1)], gsem.at[buf])

    def scopy(ids_ref, buf, r):
        return pltpu.make_async_copy(ybuf.at[buf, pl.ds(r, 1)], y_hbm.at[pl.ds(ids_ref[0, 0, r], 1)], ssem.at[buf])

    @pl.when(i == 0)
    def _():
        xbuf[...] = jnp.zeros_like(xbuf)
        n_real = y_hbm.shape[0] - N_SPARE
        for k in range(N_SPARE // DMA_UNROLL):
            init = pltpu.make_async_copy(xbuf.at[0, pl.ds(0, DMA_UNROLL)],
                                         y_hbm.at[pl.ds(n_real + k * DMA_UNROLL, DMA_UNROLL)], ssem.at[0])
            init.start()
            init.wait()
        chunked(nv, lambda r: gcopy(tok_ref, 0, r).start())

    chunked(nv_next, lambda r: gcopy(tok_nx_ref, 1 - cur, r).start())

    @pl.when(nv > 0)
    def _():
        chunked(nv, lambda r: gcopy(tok_ref, cur, r).wait())
        mm = _dot3 if precise else _bdot
        x = xbuf[cur]
        hg = mm(x, wg_ref[...])
        hu = mm(x, wu_ref[...])
        hid = (hg * jax.nn.sigmoid(hg)) * hu
        ybuf[cur] = mm(hid, wd_ref[...])
        chunked(nv, lambda r: scopy(pair_ref, cur, r).start())

    chunked(nv_prev, lambda r: scopy(pair_pv_ref, 1 - cur, r).wait())

    @pl.when(i == n - 1)
    def _():
        chunked(nv, lambda r: scopy(pair_ref, cur, r).wait())


def moe_stage(x1, tile_e, tile_nv, slot_tok, slot_pair, wg, wu, wd, layer, tm, precise):
    T, D = x1.shape
    n_tiles = tile_e.shape[0]
    F = wg.shape[-1]
    ids = lambda shift: pl.BlockSpec((1, 1, tm), lambda i, te, nv: (jnp.clip(i + shift, 0, n_tiles - 1), 0, 0),
                                     memory_space=pltpu.SMEM)
    gs = pltpu.PrefetchScalarGridSpec(
        num_scalar_prefetch=2,
        grid=(n_tiles,),
        in_specs=[ids(0), ids(1), ids(0), ids(-1),
                  pl.BlockSpec(memory_space=pl.ANY),
                  pl.BlockSpec((None, None, D, F), lambda i, te, nv: (layer, te[i], 0, 0)),
                  pl.BlockSpec((None, None, D, F), lambda i, te, nv: (layer, te[i], 0, 0)),
                  pl.BlockSpec((None, None, F, D), lambda i, te, nv: (layer, te[i], 0, 0))],
        out_specs=pl.BlockSpec(memory_space=pl.ANY),
        scratch_shapes=[pltpu.VMEM((2, tm, D), F32), pltpu.VMEM((2, tm, D), F32),
                        pltpu.SemaphoreType.DMA((2,)), pltpu.SemaphoreType.DMA((2,))],
    )
    return pl.pallas_call(
        functools.partial(_moe_kernel, precise=precise),
        grid_spec=gs,
        out_shape=jax.ShapeDtypeStruct((TOPK_INNER * T + N_SPARE, D), F32),
        compiler_params=_cparams(("arbitrary",)),
        name="moe_stage",
    )(tile_e, tile_nv, slot_tok, slot_tok, slot_pair, slot_pair, x1, wg, wu, wd)


def moe_plan(route_i, tm):
    T = route_i.shape[0]
    P = TOPK_INNER * T
    n_tiles = -(-P // tm) + N_EXPERTS
    eid = route_i[:, :TOPK_INNER].T.reshape(P)
    experts = jnp.arange(N_EXPERTS, dtype=I32)[None, :]
    onehot = (eid[:, None] == experts).astype(I32)
    counts = jnp.sum(onehot, axis=0)
    rank = jnp.sum((jnp.cumsum(onehot, axis=0) - onehot) * onehot, axis=1)
    tiles_per = (counts + tm - 1) // tm
    tile_end = jnp.cumsum(tiles_per)
    tile_start = tile_end - tiles_per
    n_used = tile_end[-1]
    ti = jnp.arange(n_tiles, dtype=I32)
    e_of = lambda t: jnp.sum((tile_end[None, :] <= t[:, None]).astype(I32), axis=1)
    used = ti < n_used
    tile_e = jnp.where(used, e_of(ti), e_of(jnp.broadcast_to(n_used - 1, (n_tiles,))))
    tile_e = jnp.clip(tile_e, 0, N_EXPERTS - 1)
    sel_e = tile_e[:, None] == experts
    local = ti - jnp.sum(jnp.where(sel_e, tile_start[None, :], 0), axis=1)
    cnt_e = jnp.sum(jnp.where(sel_e, counts[None, :], 0), axis=1)
    tile_nv = jnp.where(used, jnp.clip(cnt_e - local * tm, 0, tm), 0).astype(I32)
    slot = jnp.sum(onehot * tile_start[None, :], axis=1) * tm + rank
    s_idx = jnp.arange(n_tiles * tm, dtype=I32)
    spare = P + ((s_idx // tm) % 2) * DMA_UNROLL + s_idx % DMA_UNROLL
    slot_pair = spare.at[slot].set(jnp.arange(P, dtype=I32))
    slot_tok = jnp.where(slot_pair < P, slot_pair % T, 0)
    return tile_e.astype(I32), tile_nv, slot_tok.reshape(n_tiles, 1, tm), slot_pair.reshape(n_tiles, 1, tm)


def _ffn_sum(y0_ref, y1_ref, rw_ref):
    rw = rw_ref[...]
    return rw[:, 0:1] * y0_ref[...] + rw[:, 1:2] * y1_ref[...]


def _ple_kernel(x1_ref, y0_ref, y1_ref, rw_ref, p_ref, g1_ref, b1_ref, wp_ref, wpg_ref, g2_ref, b2_ref,
                x3_ref, x3b_ref, *, alpha):
    ffn = _ffn_sum(y0_ref, y1_ref, rw_ref)
    x2 = _ln(alpha * x1_ref[...] + ffn, g1_ref[...], b1_ref[...])
    ple = (jnp.dot(p_ref[...], wp_ref[...], preferred_element_type=F32)
           * jax.nn.sigmoid(jnp.dot(x2.astype(BF16), wpg_ref[...], preferred_element_type=F32)))
    x3 = _ln(alpha * x2 + ple, g2_ref[...], b2_ref[...])
    x3_ref[...] = x3
    x3b_ref[...] = x3.astype(BF16)


def ple_stage(x1, y, rw, pb, g1, b1, wp, wpg, g2, b2, layer, alpha):
    T, D = x1.shape
    PD = pb.shape[1]
    nt = T // TM
    row = lambda w: pl.BlockSpec((TM, w), lambda i: (i, 0))
    wsp = lambda r, c: pl.BlockSpec((None, r, c), lambda i: (layer, 0, 0))
    return pl.pallas_call(
        functools.partial(_ple_kernel, alpha=alpha),
        grid=(nt,),
        in_specs=[row(D), row(D), pl.BlockSpec((TM, D), lambda i: (nt + i, 0)), row(LANES), row(PD),
                  wsp(1, D), wsp(1, D), wsp(PD, D), wsp(D, D), wsp(1, D), wsp(1, D)],
        out_specs=[row(D), row(D)],
        out_shape=[jax.ShapeDtypeStruct((T, D), F32), jax.ShapeDtypeStruct((T, D), BF16)],
        compiler_params=_cparams(("parallel",)),
        name="ple_stage",
    )(x1, y, y, rw, pb, g1, b1, wp, wpg, g2, b2)


def _s_ple_kernel(x1_ref, y0_ref, y1_ref, rw_ref, p_ref, g1_ref, b1_ref, wp_ref, wpg_ref, g2_ref, b2_ref,
                  x3_ref, x2_s, x2k_s, pl_s, acc, *, alpha):
    k = pl.program_id(0)
    nk = x2k_s.shape[0]

    @pl.when(k == 0)
    def _():
        x2 = _ln(alpha * x1_ref[...] + _ffn_sum(y0_ref, y1_ref, rw_ref), g1_ref[...], b1_ref[...])
        x2_s[...] = x2
        for kk in range(nk):
            x2k_s[kk] = x2[:, kk * TK_S:(kk + 1) * TK_S]
        pl_s[...] = _dot3(p_ref[...], wp_ref[...])
        acc[...] = jnp.zeros_like(acc)

    acc[...] += _dot3(x2k_s[k], wpg_ref[...])

    @pl.when(k == nk - 1)
    def _():
        x3_ref[...] = _ln(alpha * x2_s[...] + pl_s[...] * jax.nn.sigmoid(acc[...]), g2_ref[...], b2_ref[...])


def s_ple(x1, y, rw, p, g1, b1, wp, wpg, g2, b2, layer, alpha):
    R, D = x1.shape
    PD = p.shape[1]
    cst = lambda r, c: pl.BlockSpec((None, r, c), lambda k: (layer, 0, 0))
    full = lambda w: pl.BlockSpec((R, w), lambda k: (0, 0))
    return pl.pallas_call(
        functools.partial(_s_ple_kernel, alpha=alpha),
        grid=(D // TK_S,),
        in_specs=[full(D), full(D), pl.BlockSpec((R, D), lambda k: (1, 0)), full(LANES), full(PD),
                  cst(1, D), cst(1, D), cst(PD, D), pl.BlockSpec((None, TK_S, D), lambda k: (layer, k, 0)),
                  cst(1, D), cst(1, D)],
        out_specs=full(D),
        out_shape=jax.ShapeDtypeStruct((R, D), F32),
        scratch_shapes=[pltpu.VMEM((R, D), F32), pltpu.VMEM((D // TK_S, R, TK_S), F32),
                        pltpu.VMEM((R, D), F32), pltpu.VMEM((R, D), F32)],
        compiler_params=_cparams(("arbitrary",)),
        name="s_ple",
    )(x1, y, y, rw, p, g1, b1, wp, wpg, g2, b2)


def _rope_tables(pos, dh, period):
    rot = dh // 4
    half = rot // 2
    inv = ROPE_THETA ** (-jnp.arange(half, dtype=F32) / half)
    ang = pos.astype(F32)[:, None] * inv[None, :]
    cos, sin = jnp.cos(ang), jnp.sin(ang)
    T = pos.shape[0]
    z = lambda n: jnp.zeros((T, n), F32)
    c = jnp.concatenate([cos, cos, jnp.ones((T, period - rot), F32)], axis=1)
    s1 = jnp.concatenate([-sin, z(period - half)], axis=1)
    s2 = jnp.concatenate([z(half), sin, z(period - rot)], axis=1)
    rep = LANES // period
    return jnp.stack([jnp.tile(c, (1, rep)), jnp.tile(s1, (1, rep)), jnp.tile(s2, (1, rep))])


def kernel(x_prompt, x_sample, cache_k, cache_v, cache_kidx, state_gla, page_table, p_prompt, p_sample, ln_in_g, ln_in_b, w_in, w_decay, b_decay, gla_norm_g, w_branch_gla, w_branch_dsa, w_out, ln_mix_g, ln_mix_b, w_group, b_group, w_router, b_router, w_gate, w_up, w_down, ln_moe_g, ln_moe_b, w_ple, w_ple_gate, ln_ple_g, ln_ple_b):
    B, S, D = x_prompt.shape
    DB, DS, _ = x_sample.shape
    depth = w_in.shape[0]
    n_past = page_table.shape[1] * cache_k.shape[2]
    Tp = B * S
    tm_in = min(TM_IN, Tp)
    assert DS == 1 and DB == SUBLANES and S % GLA_CHUNK == 0 and S % TQ == 0 and Tp % tm_in == 0
    alpha = (2.0 * depth) ** 0.25

    w_in_t = jnp.swapaxes(w_in, 1, 2)
    ckidx_t = jnp.swapaxes(cache_kidx, 2, 3)
    w_small = jnp.concatenate([w_in[..., 6160:6224], w_in[..., 3072:3088], w_in[..., 6224:6240],
                               jnp.zeros((depth, D, LANES - 96), w_in.dtype)], axis=-1)
    w_small_b = w_small.astype(BF16)
    kw = GLA_HEADS * GLA_DK
    wdp32 = jnp.zeros((depth, LANES, kw), F32).at[:, L_GA:L_GA + GLA_RANK, :].set(w_decay)
    wdp = wdp32.astype(BF16)
    bd = b_decay.reshape(depth, 1, kw)
    gn = gla_norm_g.reshape(depth, 1, GLA_DV)
    wbg, wbd, wo = w_branch_gla.astype(BF16), w_branch_dsa.astype(BF16), w_out.astype(BF16)
    w_rt = jnp.concatenate([w_group, w_router, jnp.zeros((depth, D, LANES - N_GROUPS - N_EXPERTS), F32)], axis=-1)
    wrh = w_rt.astype(BF16)
    wrl = (w_rt - wrh.astype(F32)).astype(BF16)
    b_rt = jnp.concatenate([b_group, b_router, jnp.zeros((depth, LANES - N_GROUPS - N_EXPERTS), F32)],
                           axis=-1).reshape(depth, 1, LANES)
    wp, wpg = w_ple.astype(BF16), w_ple_gate.astype(BF16)
    r3 = lambda a: a.reshape(depth, 1, D)
    g_mix, b_mix, g_moe, b_moe, g_ple, b_ple = (r3(a) for a in (ln_mix_g, ln_mix_b, ln_moe_g, ln_moe_b,
                                                                  ln_ple_g, ln_ple_b))

    pos_p = jnp.tile(jnp.arange(S, dtype=I32), B)
    pos_s = jnp.full((DB,), n_past, I32)
    tp128, tp64 = _rope_tables(pos_p, HEAD_DIM, HEAD_DIM), _rope_tables(pos_p, IDX_DIM, IDX_DIM)
    ts128, ts64 = _rope_tables(pos_s, HEAD_DIM, HEAD_DIM), _rope_tables(pos_s, IDX_DIM, IDX_DIM)

    x, xb = ln_in(x_prompt.reshape(Tp, D), ln_in_g, ln_in_b, TM)
    xs, _ = ln_in(x_sample.reshape(DB, D), ln_in_g, ln_in_b, DB)
    outs = {k: [] for k in ("kp", "vp", "kip", "sp", "ks", "vs", "kis", "ss")}
    for l in range(depth):
        h = in_proj_main(xb, w_in_t, l, tm_in)
        hs = in_proj_small(xb, w_small_b, l, tm_in, False)
        ka, va, ki, qab, kab, vab_t, qib, kie, kio = rope_stage(h, hs, tp128, tp64, TM, BF16, True)
        og, s_p = gla_prompt(h, hs, wdp[l], bd[l], gn[l], B, S)
        att = dsa_prompt(qab, qib, hs, kab, vab_t, kie, kio, B, S)
        x1, route_i, route_w = mix_stage(og, att, h, x, wbg, wbd, wo, g_mix, b_mix, wrh, wrl, b_rt, l, alpha)
        y = moe_stage(x1, *moe_plan(route_i, TM), w_gate, w_up, w_down, l, TM, False)
        x, xb = ple_stage(x1, y, route_w, p_prompt[l].reshape(Tp, -1).astype(BF16), g_moe, b_moe, wp, wpg,
                          g_ple, b_ple, l, alpha)
        h_s = s_in_proj_main(xs, w_in_t, l)
        hs_s = in_proj_small(xs, w_small, l, DB, True)
        ka_s, va_s, ki_s, qa_s, _, _, qib_s, _, _ = rope_stage(h_s, hs_s, ts128, ts64, DB, F32, False)
        og_s, s_s = gla_sample(h_s, hs_s, wdp32[l], bd[l], gn[l], state_gla, l)
        at_s = dsa_sample(page_table, qib_s.reshape(DB, IDX_HEADS, IDX_DIM),
                          (hs_s[:, L_IW:L_IW + IDX_HEADS] * (IDX_HEADS ** -0.5)).reshape(DB, IDX_HEADS, 1),
                          ki_s.reshape(DB, 1, IDX_DIM),
                          qa_s.reshape(DB, ATT_HEADS, HEAD_DIM),
                          ka_s.reshape(DB, KV_HEADS, HEAD_DIM), va_s.reshape(DB, KV_HEADS, HEAD_DIM),
                          ckidx_t, cache_k, cache_v, l)
        merged_s = s_merge(og_s, at_s.reshape(DB, -1), h_s, w_branch_gla, w_branch_dsa, l)
        x1_s, ri_s, rw_s = s_outproj(merged_s, xs, w_out, g_mix, b_mix, w_rt, b_rt, l, alpha)
        y_s = moe_stage(x1_s, *moe_plan(ri_s, DB), w_gate, w_up, w_down, l, DB, True)
        xs = s_ple(x1_s, y_s, rw_s, p_sample[l].reshape(DB, -1), g_moe, b_moe, w_ple, w_ple_gate, g_ple, b_ple, l, alpha)
        for k, v in zip(outs, (ka, va, ki, s_p, ka_s, va_s, ki_s, s_s)):
            outs[k].append(v)

    st = {k: jnp.stack(v) for k, v in outs.items()}
    return (x.reshape(B, S, D), xs.reshape(DB, DS, D),
            st["kp"].reshape(depth, B, S, KV_HEADS, HEAD_DIM), st["vp"].reshape(depth, B, S, KV_HEADS, HEAD_DIM),
            st["kip"].reshape(depth, B, S, IDX_DIM), st["sp"],
            st["ks"].reshape(depth, DB, DS, KV_HEADS, HEAD_DIM), st["vs"].reshape(depth, DB, DS, KV_HEADS, HEAD_DIM),
            st["kis"].reshape(depth, DB, DS, IDX_DIM), st["ss"])
```

```python
import functools

import jax
import jax.numpy as jnp
from jax import lax
from jax.experimental import pallas as pl
from jax.experimental.pallas import tpu as pltpu

F32 = jnp.float32
BF16 = jnp.bfloat16
I32 = jnp.int32

GLA_HEADS = 4
GLA_DK = 128
GLA_DV = 256
GLA_RANK = 16
GLA_TAU = 16.0
ATT_HEADS = 8
KV_HEADS = 4
HEAD_DIM = 128
IDX_HEADS = 16
IDX_DIM = 64
TOPK_MAX = 256
ROPE_THETA = 500000.0
N_GROUPS = 4
EXP_PER_GROUP = 8
N_EXPERTS = N_GROUPS * EXP_PER_GROUP
TOPK_INNER = 2
LN_EPS = 1e-5

LANES = 128
SUBLANES = 8
TM_IN = 1024
TN_IN = 1024
TM = 256
GLA_CHUNK = 128
DECAY_FACTOR_LIMIT = 80.0
TQ = 128
KEY_BUCKET = 512
SCORE_PAGES = 8
RADIX_BITS = 4
DMA_UNROLL = 8
N_SPARE = 2 * DMA_UNROLL
TK_S = 512
VMEM_LIMIT = 56 * 1024 * 1024
INT_MIN = -(2 ** 31)

C_GQ, C_GK, C_GV, C_GR = 0, 512, 1024, 2048
C_AQ, C_AK, C_AV, C_IQ = 3072, 4096, 4608, 5120
C_GG, C_GD = 6144, 8192
N_MAIN = 10240
W_IN_TILE_OFFSETS = (0, 1024, 2048, 3088, 4112, 5136, 6240, 7264, 8288, 9312)
L_IK, L_GA, L_IW = 0, 64, 80


def _cparams(sem):
    return pltpu.CompilerParams(dimension_semantics=sem, vmem_limit_bytes=VMEM_LIMIT)


def _bdot(a, b):
    return jnp.dot(a.astype(BF16), b.astype(BF16), preferred_element_type=F32)


def _bdot_nt(a, b):
    return lax.dot_general(a.astype(BF16), b.astype(BF16), (((1,), (1,)), ((), ())),
                           preferred_element_type=F32)


def _hilo(x):
    hi = x.astype(BF16)
    return hi, (x - hi.astype(F32)).astype(BF16)


def _dot3(a, w):
    ah, al = _hilo(a)
    wh, wl = _hilo(w)
    d = lambda p, q: jnp.dot(p, q, preferred_element_type=F32)
    return d(ah, wh) + (d(al, wh) + d(ah, wl))


def _dot3_nt(a, wt):
    ah, al = _hilo(a)
    wh, wl = _hilo(wt)
    d = lambda p, q: lax.dot_general(p, q, (((1,), (1,)), ((), ())), preferred_element_type=F32)
    return d(ah, wh) + (d(al, wh) + d(ah, wl))


def _split3(x):
    hi = x.astype(BF16)
    r = x - hi.astype(F32)
    mid = r.astype(BF16)
    lo = (r - mid.astype(F32)).astype(BF16)
    return hi, mid, lo


def _ln(x, g, b):
    mu = jnp.mean(x, axis=-1, keepdims=True)
    xc = x - mu
    var = jnp.mean(xc * xc, axis=-1, keepdims=True)
    return xc * lax.rsqrt(var + LN_EPS) * g + b


def _col_to_row(col, n):
    eye = lax.broadcasted_iota(I32, (n, n), 0) == lax.broadcasted_iota(I32, (n, n), 1)
    return jnp.sum(jnp.where(eye, col, 0.0), axis=0, keepdims=True)


def _row_to_col(row, n):
    eye = lax.broadcasted_iota(I32, (n, n), 0) == lax.broadcasted_iota(I32, (n, n), 1)
    return jnp.sum(jnp.where(eye, row, 0.0), axis=1, keepdims=True)


def _sort_key(x):
    bits = lax.bitcast_convert_type(x, I32)
    return jnp.where(bits < 0, bits ^ jnp.int32(0x7FFFFFFF), bits)


def _ln_in_kernel(x_ref, g_ref, b_ref, o_ref, ob_ref):
    y = _ln(x_ref[...], g_ref[...], b_ref[...])
    o_ref[...] = y
    ob_ref[...] = y.astype(BF16)


def ln_in(x, g, b, tm):
    T, D = x.shape
    return pl.pallas_call(
        _ln_in_kernel,
        grid=(T // tm,),
        in_specs=[pl.BlockSpec((tm, D), lambda i: (i, 0)),
                  pl.BlockSpec((1, D), lambda i: (0, 0)),
                  pl.BlockSpec((1, D), lambda i: (0, 0))],
        out_specs=[pl.BlockSpec((tm, D), lambda i: (i, 0)),
                   pl.BlockSpec((tm, D), lambda i: (i, 0))],
        out_shape=[jax.ShapeDtypeStruct((T, D), F32), jax.ShapeDtypeStruct((T, D), BF16)],
        compiler_params=_cparams(("parallel",)),
        name="ln_in",
    )(x, g.reshape(1, D), b.reshape(1, D))


def _in_proj_kernel(off_ref, x_ref, wt_ref, o_ref, w_s):
    @pl.when(pl.program_id(1) == 0)
    def _():
        w_s[...] = wt_ref[0].T.astype(BF16)

    o_ref[...] = jnp.dot(x_ref[...], w_s[...], preferred_element_type=F32)


def _s_in_proj_kernel(off_ref, x_ref, wt_ref, o_ref):
    o_ref[...] = _dot3_nt(x_ref[...], wt_ref[0])


def _w_in_tile_spec(layer, D):
    return pl.BlockSpec((pl.Element(1), pl.Element(TN_IN), pl.Element(D)),
                        lambda j, *a: (layer, pl.multiple_of(a[-1][j], SUBLANES), 0))


def in_proj_main(xb, w_t, layer, tm):
    T, D = xb.shape
    offs = jnp.asarray(W_IN_TILE_OFFSETS, I32)
    gs = pltpu.PrefetchScalarGridSpec(
        num_scalar_prefetch=1,
        grid=(len(W_IN_TILE_OFFSETS), T // tm),
        in_specs=[pl.BlockSpec((tm, D), lambda j, i, off: (i, 0)), _w_in_tile_spec(layer, D)],
        out_specs=pl.BlockSpec((tm, TN_IN), lambda j, i, off: (i, j)),
        scratch_shapes=[pltpu.VMEM((D, TN_IN), BF16)],
    )
    return pl.pallas_call(
        _in_proj_kernel,
        grid_spec=gs,
        out_shape=jax.ShapeDtypeStruct((T, N_MAIN), F32),
        compiler_params=_cparams(("parallel", "arbitrary")),
        name="in_proj_main",
    )(offs, xb, w_t)


def s_in_proj_main(x, w_t, layer):
    R, D = x.shape
    offs = jnp.asarray(W_IN_TILE_OFFSETS, I32)
    gs = pltpu.PrefetchScalarGridSpec(
        num_scalar_prefetch=1,
        grid=(len(W_IN_TILE_OFFSETS),),
        in_specs=[pl.BlockSpec((R, D), lambda j, off: (0, 0)), _w_in_tile_spec(layer, D)],
        out_specs=pl.BlockSpec((R, TN_IN), lambda j, off: (0, j)),
    )
    return pl.pallas_call(
        _s_in_proj_kernel,
        grid_spec=gs,
        out_shape=jax.ShapeDtypeStruct((R, N_MAIN), F32),
        compiler_params=_cparams(("parallel",)),
        name="s_in_proj_main",
    )(offs, x, w_t)


def _mm_kernel(x_ref, w_ref, o_ref):
    o_ref[...] = jnp.dot(x_ref[...], w_ref[...], preferred_element_type=F32)


def _s_mm_kernel(x_ref, w_ref, o_ref):
    o_ref[...] = _dot3(x_ref[...], w_ref[...])


def in_proj_small(x, w, layer, tm, precise):
    T, D = x.shape
    N = w.shape[-1]
    return pl.pallas_call(
        _s_mm_kernel if precise else _mm_kernel,
        grid=(T // tm,),
        in_specs=[pl.BlockSpec((tm, D), lambda i: (i, 0)),
                  pl.BlockSpec((None, D, N), lambda i: (layer, 0, 0))],
        out_specs=pl.BlockSpec((tm, N), lambda i: (i, 0)),
        out_shape=jax.ShapeDtypeStruct((T, N), F32),
        compiler_params=_cparams(("parallel",)),
        name="in_proj_small",
    )(x, w)


def _rope_tile(x, c, s1, s2, shift):
    w = x.shape[-1]
    xm = pltpu.roll(x, w - shift, axis=1)
    xp = pltpu.roll(x, shift, axis=1)
    return x * c + xm * s1 + xp * s2


def _rope_kernel(aq_ref, ak_ref, av_ref, iq_ref, hs_ref, t128_ref, t64_ref,
                 ka_ref, va_ref, ki_ref, qa_ref, kab_ref, vab_ref, qib_ref, kie_ref, kio_ref):
    c1, s1a, s1b = t128_ref[0], t128_ref[1], t128_ref[2]
    c2, s2a, s2b = t64_ref[0], t64_ref[1], t64_ref[2]
    for hd in range(ATT_HEADS):
        sl = slice(hd * HEAD_DIM, (hd + 1) * HEAD_DIM)
        qa_ref[:, sl] = _rope_tile(aq_ref[:, sl], c1, s1a, s1b, HEAD_DIM // 8).astype(qa_ref.dtype)
    for hd in range(KV_HEADS):
        sl = slice(hd * HEAD_DIM, (hd + 1) * HEAD_DIM)
        kr = _rope_tile(ak_ref[:, sl], c1, s1a, s1b, HEAD_DIM // 8)
        ka_ref[:, sl] = kr
        kab_ref[:, sl] = kr.astype(BF16)
    v = av_ref[...]
    va_ref[...] = v
    if vab_ref.shape[0] == v.shape[0]:
        vab_ref[...] = v.astype(BF16)
    else:
        vab_ref[...] = v.T.astype(BF16)
    for pr in range(IDX_HEADS * IDX_DIM // LANES):
        sl = slice(pr * LANES, (pr + 1) * LANES)
        qr = _rope_tile(iq_ref[:, sl], c2, s2a, s2b, IDX_DIM // 8) * (IDX_DIM ** -0.5)
        qib_ref[:, sl] = qr.astype(qib_ref.dtype)
    kir = _rope_tile(hs_ref[...], c2, s2a, s2b, IDX_DIM // 8)
    ki_ref[...] = kir[:, :IDX_DIM]
    lane = lax.broadcasted_iota(I32, kir.shape, 1)
    ke = jnp.where(lane < IDX_DIM, kir, 0.0)
    kie_ref[...] = ke.astype(BF16)
    kio_ref[...] = pltpu.roll(ke, IDX_DIM, axis=1).astype(BF16)


def rope_stage(h, hs, t128, t64, tm, q_dtype, v_transposed):
    T = h.shape[0]
    aw, kw, iw = ATT_HEADS * HEAD_DIM, KV_HEADS * HEAD_DIM, IDX_HEADS * IDX_DIM
    row = lambda w, c: pl.BlockSpec((tm, w), lambda i: (i, c))
    tab = pl.BlockSpec((3, tm, LANES), lambda i: (0, i, 0))
    vb_spec = pl.BlockSpec((kw, tm), lambda i: (0, i)) if v_transposed else row(kw, 0)
    vb_shape = (kw, T) if v_transposed else (T, kw)
    return pl.pallas_call(
        _rope_kernel,
        grid=(T // tm,),
        in_specs=[row(aw, C_AQ // aw), row(kw, C_AK // kw), row(kw, C_AV // kw), row(iw, C_IQ // iw),
                  row(LANES, 0), tab, tab],
        out_specs=[row(kw, 0), row(kw, 0), row(IDX_DIM, 0), row(aw, 0), row(kw, 0), vb_spec,
                   row(iw, 0), row(LANES, 0), row(LANES, 0)],
        out_shape=[jax.ShapeDtypeStruct((T, kw), F32), jax.ShapeDtypeStruct((T, kw), F32),
                   jax.ShapeDtypeStruct((T, IDX_DIM), F32), jax.ShapeDtypeStruct((T, aw), q_dtype),
                   jax.ShapeDtypeStruct((T, kw), BF16), jax.ShapeDtypeStruct(vb_shape, BF16),
                   jax.ShapeDtypeStruct((T, iw), q_dtype), jax.ShapeDtypeStruct((T, LANES), BF16),
                   jax.ShapeDtypeStruct((T, LANES), BF16)],
        compiler_params=_cparams(("parallel",)),
        name="rope_stage",
    )(h, h, h, h, hs, t128, t64)


def _log_decay(hs, wdp_ref, bd_ref, precise=False):
    z = (_dot3 if precise else _bdot)(hs, wdp_ref[...]) + bd_ref[...]
    return (jnp.minimum(z, 0.0) - jnp.log1p(jnp.exp(-jnp.abs(z)))) * (1.0 / GLA_TAU)


def _gla_out(o, gn, gr):
    on = o * lax.rsqrt(jnp.mean(o * o, axis=-1, keepdims=True) + LN_EPS) * gn
    return on * (gr * jax.nn.sigmoid(gr))


def _gla_prompt_kernel(hq_ref, hk_ref, hv_ref, hr_ref, hs_ref, wdp_ref, bd_ref, gn_ref,
                       og_ref, sfin_ref, s_ref, att_s, b_s, q_s):
    c = pl.program_id(1)
    C = GLA_CHUNK

    @pl.when(c == 0)
    def _():
        s_ref[...] = jnp.zeros_like(s_ref)

    la_all = _log_decay(hs_ref[...], wdp_ref, bd_ref)
    r_i = lax.broadcasted_iota(I32, (C, C), 0)
    c_i = lax.broadcasted_iota(I32, (C, C), 1)
    causal = r_i >= c_i
    tri = jnp.where(causal, 1.0, 0.0).astype(BF16)
    s_idx = lax.broadcasted_iota(I32, (C, 1), 0)
    gn = gn_ref[...]
    bs = []
    for hd in range(GLA_HEADS):
        hi, mid, lo = _split3(la_all[:, hd * GLA_DK:(hd + 1) * GLA_DK])
        bs.append(jnp.dot(tri, hi, preferred_element_type=F32)
                  + jnp.dot(tri, mid, preferred_element_type=F32)
                  + jnp.dot(tri, lo, preferred_element_type=F32))
    factor_ok = jnp.max(-jnp.concatenate([b[C - 1:C, :] for b in bs], axis=1)) <= DECAY_FACTOR_LIMIT

    def att_factorised(hd, b, qe, k):
        return jnp.where(causal, _bdot_nt(qe, k * jnp.exp(-b)), 0.0)

    def att_by_rows(hd, b, qe, k):
        b_s[...] = b
        q_s[...] = hq_ref[:, hd * GLA_DK:(hd + 1) * GLA_DK] * (GLA_DK ** -0.5)

        def row(t, carry):
            d = jnp.exp(jnp.minimum(b_s[pl.ds(t, 1), :] - b, 0.0))
            col = jnp.sum(k * d * q_s[pl.ds(t, 1), :], axis=1, keepdims=True)
            col = jnp.where(s_idx <= t, col, 0.0)
            att_s[pl.ds(t, 1), :] = _col_to_row(jnp.broadcast_to(col, (C, C)), C)
            return carry

        lax.fori_loop(0, C, row, 0)
        return att_s[...]

    def heads(att_fn):
        for hd in range(GLA_HEADS):
            ks = slice(hd * GLA_DK, (hd + 1) * GLA_DK)
            vs = slice(hd * GLA_DV, (hd + 1) * GLA_DV)
            b = bs[hd]
            eb = jnp.exp(b)
            qe = hq_ref[:, ks] * (GLA_DK ** -0.5) * eb
            k = hk_ref[:, ks]
            v = hv_ref[:, vs]
            s_old = s_ref[hd]
            o = _bdot(att_fn(hd, b, qe, k), v) + _bdot(qe, s_old)
            og_ref[:, vs] = _gla_out(o, gn, hr_ref[:, vs]).astype(BF16)
            kdec = k * jnp.exp(b[C - 1:C, :] - b)
            upd = lax.dot_general(kdec.astype(BF16), v.astype(BF16), (((0,), (0,)), ((), ())),
                                  preferred_element_type=F32)
            s_new = s_old * _row_to_col(jnp.broadcast_to(eb[C - 1:C, :], (GLA_DK, GLA_DK)), GLA_DK) + upd
            s_ref[hd] = s_new
            sfin_ref[0, hd] = s_new

    @pl.when(factor_ok)
    def _():
        heads(att_factorised)

    @pl.when(jnp.logical_not(factor_ok))
    def _():
        heads(att_by_rows)


def gla_prompt(h, hs, wdp, bd, gn, B, S):
    nC = S // GLA_CHUNK
    kw, vw = GLA_HEADS * GLA_DK, GLA_HEADS * GLA_DV
    row = lambda w, col: pl.BlockSpec((GLA_CHUNK, w), lambda b, c: (b * nC + c, col))
    cst = lambda shp: pl.BlockSpec(shp, lambda b, c: (0,) * len(shp))
    return pl.pallas_call(
        _gla_prompt_kernel,
        grid=(B, nC),
        in_specs=[row(kw, C_GQ // kw), row(kw, C_GK // kw), row(vw, C_GV // vw), row(vw, C_GR // vw),
                  row(LANES, 0), cst((LANES, kw)), cst((1, kw)), cst((1, GLA_DV))],
        out_specs=[row(vw, 0),
                   pl.BlockSpec((1, GLA_HEADS, GLA_DK, GLA_DV), lambda b, c: (b, 0, 0, 0))],
        out_shape=[jax.ShapeDtypeStruct((B * S, vw), BF16),
                   jax.ShapeDtypeStruct((B, GLA_HEADS, GLA_DK, GLA_DV), F32)],
        scratch_shapes=[pltpu.VMEM((GLA_HEADS, GLA_DK, GLA_DV), F32),
                        pltpu.VMEM((GLA_CHUNK, GLA_CHUNK), F32), pltpu.VMEM((GLA_CHUNK, GLA_DK), F32),
                        pltpu.VMEM((GLA_CHUNK, GLA_DK), F32)],
        compiler_params=_cparams(("arbitrary", "arbitrary")),
        name="gla_prompt",
    )(h, h, h, h, hs, wdp, bd, gn)


def _gla_sample_kernel(hq_ref, hk_ref, hv_ref, hr_ref, hs_ref, wdp_ref, bd_ref, gn_ref, s0_ref,
                       og_ref, s1_ref):
    b = pl.program_id(0)
    la_all = _log_decay(hs_ref[...], wdp_ref, bd_ref, precise=True)
    gn = gn_ref[...]
    rowsel = lax.broadcasted_iota(I32, (hq_ref.shape[0], 1), 0) == b

    def pick(x):
        return jnp.sum(jnp.where(rowsel, x, 0.0), axis=0, keepdims=True)

    for hd in range(GLA_HEADS):
        ks = slice(hd * GLA_DK, (hd + 1) * GLA_DK)
        vs = slice(hd * GLA_DV, (hd + 1) * GLA_DV)
        a_row = jnp.exp(pick(la_all[:, ks]))
        q_row = pick(hq_ref[:, ks]) * (GLA_DK ** -0.5)
        k_row = pick(hk_ref[:, ks])
        v_row = pick(hv_ref[:, vs])
        bc = lambda r: _row_to_col(jnp.broadcast_to(r, (GLA_DK, GLA_DK)), GLA_DK)
        s_new = s0_ref[0, hd] * bc(a_row) + bc(k_row) * v_row
        s1_ref[0, hd] = s_new
        o = jnp.sum(bc(q_row) * s_new, axis=0, keepdims=True)
        og_ref[pl.ds(b, 1), vs] = _gla_out(o, gn, pick(hr_ref[:, vs]))


def gla_sample(h, hs, wdp, bd, gn, state, layer):
    DB = h.shape[0]
    kw, vw = GLA_HEADS * GLA_DK, GLA_HEADS * GLA_DV
    row = lambda w, col: pl.BlockSpec((DB, w), lambda b: (0, col))
    cst = lambda shp: pl.BlockSpec(shp, lambda b: (0,) * len(shp))
    return pl.pallas_call(
        _gla_sample_kernel,
        grid=(DB,),
        in_specs=[row(kw, C_GQ // kw), row(kw, C_GK // kw), row(vw, C_GV // vw), row(vw, C_GR // vw),
                  row(LANES, 0), cst((LANES, kw)), cst((1, kw)), cst((1, GLA_DV)),
                  pl.BlockSpec((None, 1, GLA_HEADS, GLA_DK, GLA_DV), lambda b: (layer, b, 0, 0, 0))],
        out_specs=[pl.BlockSpec((DB, vw), lambda b: (0, 0)),
                   pl.BlockSpec((1, GLA_HEADS, GLA_DK, GLA_DV), lambda b: (b, 0, 0, 0))],
        out_shape=[jax.ShapeDtypeStruct((DB, vw), F32),
                   jax.ShapeDtypeStruct((DB, GLA_HEADS, GLA_DK, GLA_DV), F32)],
        compiler_params=_cparams(("arbitrary",)),
        name="gla_sample",
    )(h, h, h, h, hs, wdp, bd, gn, state)


def _colreduce(x, red, pair, groups=8):
    n, C = x.shape
    x3 = x.reshape(n // SUBLANES, SUBLANES, C)
    rows = x3.shape[0] // groups
    parts = [red(x3[g * rows:(g + 1) * rows], axis=0) for g in range(groups)]
    while len(parts) > 1:
        parts = [pair(parts[j], parts[j + 1]) for j in range(0, len(parts), 2)]
    return red(parts[0], axis=0, keepdims=True)


def _colsum(x):
    return _colreduce(x, jnp.sum, jnp.add)


def _colmax(x):
    return _colreduce(x, jnp.max, jnp.maximum)


def _kth_threshold_t(key, topk):
    C = key.shape[1]

    def body(i, t):
        cand = t + jnp.left_shift(jnp.int32(1), 31 - i)
        cnt = _colsum(jnp.where(key >= cand, 1.0, 0.0))
        return jnp.where(cnt >= float(topk), cand, t)

    return lax.fori_loop(0, 32, body, jnp.full((1, C), INT_MIN, I32))


def _dsa_prompt_body(qb, nk, qa_ref, qi_ref, hs_ref, ka_ref, vt_ref, kie_ref, kio_ref, o_ref, sel_s, gt_s, topk):
    nt = (((1,), (1,)), ((), ()))
    hs_t = hs_ref[...].T
    acc = jnp.zeros((nk, TQ), F32)
    for hd in range(IDX_HEADS):
        pr = hd // 2
        q = qi_ref[:, pr * LANES:(pr + 1) * LANES]
        kk = kie_ref[:nk, :] if hd % 2 == 0 else kio_ref[:nk, :]
        s = lax.dot_general(kk, q, nt, preferred_element_type=F32)
        w = hs_t[L_IW + hd:L_IW + hd + 1, :] * (IDX_HEADS ** -0.5)
        acc = acc + w * jnp.maximum(s, 0.0)
    kpos = lax.broadcasted_iota(I32, (nk, TQ), 0)
    qpos = qb * TQ + lax.broadcasted_iota(I32, (nk, TQ), 1)
    vis = kpos <= qpos
    if nk > topk:
        key = jnp.where(vis, _sort_key(acc), INT_MIN)
        thr = _kth_threshold_t(key, topk)
        ge = jnp.where(jnp.logical_and(key >= thr, vis), 1.0, 0.0)
        sel_s[:nk, :] = ge
        n_ge = _colsum(ge)

        @pl.when(jnp.max(n_ge) > float(topk))
        def _():
            gt = jnp.where(jnp.logical_and(key > thr, vis), 1.0, 0.0)
            gt_s[:nk, :] = gt
            need = float(topk) - _colsum(gt)
            r_b = lax.broadcasted_iota(I32, (LANES, LANES), 0)
            c_b = lax.broadcasted_iota(I32, (LANES, LANES), 1)
            below = jnp.where(c_b < r_b, 1.0, 0.0).astype(BF16)

            def block(kb, seen):
                rows = pl.ds(pl.multiple_of(kb * LANES, LANES), LANES)
                g = gt_s[rows, :]
                e = sel_s[rows, :] - g
                rank = jnp.dot(below, e.astype(BF16), preferred_element_type=F32) + seen
                sel_s[rows, :] = g + jnp.where(rank < need, e, 0.0)
                return seen + jnp.sum(e, axis=0, keepdims=True)

            lax.fori_loop(0, nk // LANES, block, jnp.zeros((1, TQ), F32))

        sel = sel_s[:nk, :] > 0.5
    else:
        sel = vis
    G = ATT_HEADS // KV_HEADS
    for n in range(KV_HEADS):
        ksl = slice(n * HEAD_DIM, (n + 1) * HEAD_DIM)
        kn = ka_ref[:nk, ksl]
        vt = vt_ref[ksl, :nk]
        for g in range(G):
            hsl = slice((n * G + g) * HEAD_DIM, (n * G + g + 1) * HEAD_DIM)
            s = lax.dot_general(kn, qa_ref[:, hsl], nt, preferred_element_type=F32) * (HEAD_DIM ** -0.5)
            s = jnp.where(sel, s, -jnp.inf)
            m = _colmax(s)
            p = jnp.exp(s - m)
            l = _colsum(p)
            o_t =jnp.dot(vt, p.astype(BF16), preferred_element_type=F32) / l
            o_ref[:, hsl] = o_t.T.astype(BF16)


def _dsa_prompt_kernel(qa_ref, qi_ref, hs_ref, ka_ref, va_ref, kie_ref, kio_ref, o_ref, sel_s, gt_s,
                       *, topk, bucket):
    qb = pl.program_id(1)
    S = ka_ref.shape[0]
    for bi in range(S // bucket):
        @pl.when(qb // (bucket // TQ) == bi)
        def _():
            _dsa_prompt_body(qb, (bi + 1) * bucket, qa_ref, qi_ref, hs_ref, ka_ref, va_ref, kie_ref, kio_ref,
                             o_ref, sel_s, gt_s, topk)


def dsa_prompt(qab, qib, hs, kab, vab_t, kie, kio, B, S):
    nq = S // TQ
    aw, kw, iw = ATT_HEADS * HEAD_DIM, KV_HEADS * HEAD_DIM, IDX_HEADS * IDX_DIM
    topk = min(TOPK_MAX, S // 4)
    bucket = min(KEY_BUCKET, S)
    assert S % bucket == 0 and bucket % TQ == 0
    qrow = lambda w: pl.BlockSpec((TQ, w), lambda b, q: (b * nq + q, 0))
    seq = lambda w: pl.BlockSpec((S, w), lambda b, q: (b, 0))
    return pl.pallas_call(
        functools.partial(_dsa_prompt_kernel, topk=topk, bucket=bucket),
        grid=(B, nq),
        in_specs=[qrow(aw), qrow(iw), qrow(LANES), seq(kw), pl.BlockSpec((kw, S), lambda b, q: (0, b)),
                  seq(LANES), seq(LANES)],
        out_specs=qrow(aw),
        out_shape=jax.ShapeDtypeStruct((B * S, aw), BF16),
        scratch_shapes=[pltpu.VMEM((S, TQ), F32), pltpu.VMEM((S, TQ), F32)],
        compiler_params=_cparams(("parallel", "parallel")),
        name="dsa_prompt",
    )(qab, qib, hs, kab, vab_t, kie, kio)


def _dsa_sample_kernel(pt_ref, qi_ref, w_ref, kin_ref, qa_ref, kn_ref, vn_ref,
                       ckidx_hbm, ck_hbm, cv_hbm, o_ref,
                       kbuf, sc_ref, idxv_ref, idxs_ref, kg_ref, vg_ref, sem_i, sem_x, sem_k, sem_v,
                       *, layer, n_pages, page, topk):
    b = pl.program_id(0)
    NP, PG, K = n_pages, page, topk

    def icopy(p):
        return pltpu.make_async_copy(ckidx_hbm.at[layer, pt_ref[b, p]], kbuf.at[p], sem_i)

    lax.fori_loop(0, NP, lambda p, c: (icopy(p).start(), c)[1], 0)
    lax.fori_loop(0, NP, lambda p, c: (icopy(p).wait(), c)[1], 0)

    q = qi_ref[0]
    wcol = w_ref[0]

    def score_chunk(c, carry):
        pages = kbuf[pl.ds(c * SCORE_PAGES, SCORE_PAGES)]
        kc = jnp.concatenate([pages[j] for j in range(SCORE_PAGES)], axis=1)
        s = _dot3(q, kc)
        r = jnp.sum(wcol * jnp.maximum(s, 0.0), axis=0, keepdims=True)
        for j in range(SCORE_PAGES):
            sc_ref[pl.ds(c * SCORE_PAGES + j, 1), :] = r[:, j * PG:(j + 1) * PG]
        return carry

    lax.fori_loop(0, NP // SCORE_PAGES, score_chunk, 0)
    s_new = jnp.sum(q * kin_ref[0], axis=1, keepdims=True)
    s_new = jnp.sum(wcol * jnp.maximum(s_new, 0.0), axis=0, keepdims=True)

    key = _sort_key(sc_ref[...])
    key_new = _sort_key(s_new)

    def body(i, t):
        shift = 32 - RADIX_BITS * (i + 1)
        digit = jnp.zeros((1, 1), I32)
        for d in range(1, 2 ** RADIX_BITS):
            cand = t + jnp.left_shift(jnp.int32(d), shift)
            cnt = jnp.sum(jnp.sum(jnp.where(key >= cand, 1.0, 0.0), axis=0, keepdims=True), axis=1, keepdims=True)
            cnt = cnt + jnp.where(key_new >= cand, 1.0, 0.0)
            digit = digit + jnp.where(cnt >= float(K), 1, 0)
        return t + jnp.left_shift(digit, shift)

    thr = lax.fori_loop(0, 32 // RADIX_BITS, body, jnp.full((1, 1), INT_MIN, I32))
    gt = jnp.where(key > thr, 1.0, 0.0)
    eq = jnp.where(key == thr, 1.0, 0.0)
    n_gt = (jnp.sum(jnp.sum(gt, axis=1, keepdims=True), axis=0, keepdims=True)
            + jnp.where(key_new > thr, 1.0, 0.0))
    need = float(K) - n_gt

    r_p = lax.broadcasted_iota(I32, (PG, PG), 0)
    c_p = lax.broadcasted_iota(I32, (PG, PG), 1)
    ut = jnp.where(r_p <= c_p, 1.0, 0.0).astype(BF16)
    r_n = lax.broadcasted_iota(I32, (NP, NP), 0)
    c_n = lax.broadcasted_iota(I32, (NP, NP), 1)
    slt = jnp.where(c_n < r_n, 1.0, 0.0).astype(BF16)

    def prefix(m):
        cs = jnp.dot(m.astype(BF16), ut, preferred_element_type=F32)
        tot = cs[:, PG - 1:PG]
        off = jnp.dot(slt, jnp.broadcast_to(tot, (NP, PG)).astype(BF16), preferred_element_type=F32)[:, 0:1]
        return cs, off, tot

    cs_e, off_e, tot_e = prefix(eq)
    tie_rank = off_e + cs_e - eq
    sel = jnp.maximum(gt, jnp.where(tie_rank < need, eq, 0.0))
    n_eq_past = jnp.sum(tot_e, axis=0, keepdims=True)
    sel_new = jnp.logical_or(key_new > thr, jnp.logical_and(key_new == thr, n_eq_past < need))

    cs, off, tot = prefix(sel)
    n_past_sel = jnp.sum(tot, axis=0, keepdims=True)
    ci_row = _col_to_row(jnp.broadcast_to(off + tot, (NP, NP)), NP)
    off_row = _col_to_row(jnp.broadcast_to(off, (NP, NP)), NP)
    jcol = lax.broadcasted_iota(I32, (K, 1), 0).astype(F32)
    page_of = jnp.sum(jnp.where(ci_row <= jcol, 1.0, 0.0), axis=1, keepdims=True)
    lane_p = lax.broadcasted_iota(I32, (K, NP), 1).astype(F32)
    onehot = jnp.where(lane_p == page_of, 1.0, 0.0)
    off_j = jnp.sum(onehot * off_row, axis=1, keepdims=True)
    lr = jcol - off_j
    cs_row = jnp.dot(onehot.astype(BF16), cs.astype(BF16), preferred_element_type=F32)
    sel_row = jnp.dot(onehot.astype(BF16), sel.astype(BF16), preferred_element_type=F32)
    lane_o = lax.broadcasted_iota(I32, (K, PG), 1).astype(F32)
    hit = jnp.logical_and(cs_row - 1.0 == lr, sel_row > 0.5)
    off_of = jnp.sum(jnp.where(hit, lane_o, 0.0), axis=1, keepdims=True)
    valid_j = jcol < n_past_sel
    page_of = jnp.where(valid_j, page_of, 0.0)
    off_of = jnp.where(valid_j, off_of, 0.0)
    idxv_ref[...] = jnp.zeros_like(idxv_ref)
    nh = K // LANES
    for hh in range(nh):
        pg_r = _col_to_row(jnp.broadcast_to(page_of[hh * LANES:(hh + 1) * LANES], (LANES, LANES)), LANES)
        of_r = _col_to_row(jnp.broadcast_to(off_of[hh * LANES:(hh + 1) * LANES], (LANES, LANES)), LANES)
        idxv_ref[hh:hh + 1, :] = pg_r.astype(I32)
        idxv_ref[nh + hh:nh + hh + 1, :] = of_r.astype(I32)
    xcp = pltpu.make_async_copy(idxv_ref, idxs_ref, sem_x)
    xcp.start()
    xcp.wait()

    def kcopy(j):
        pg = idxs_ref[j // LANES, j % LANES]
        of = idxs_ref[nh + j // LANES, j % LANES]
        phys = pt_ref[b, pg]
        return (pltpu.make_async_copy(ck_hbm.at[layer, phys, of], kg_ref.at[j], sem_k),
                pltpu.make_async_copy(cv_hbm.at[layer, phys, of], vg_ref.at[j], sem_v))

    def start_j(c, carry):
        for u in range(DMA_UNROLL):
            ck, cv = kcopy(c * DMA_UNROLL + u)
            ck.start()
            cv.start()
        return carry

    def wait_j(c, carry):
        for u in range(DMA_UNROLL):
            ck, cv = kcopy(c * DMA_UNROLL + u)
            ck.wait()
            cv.wait()
        return carry

    lax.fori_loop(0, K // DMA_UNROLL, start_j, 0)
    lax.fori_loop(0, K // DMA_UNROLL, wait_j, 0)

    is_new = jnp.logical_and(jnp.logical_not(valid_j), sel_new)
    G = ATT_HEADS // KV_HEADS
    for n in range(KV_HEADS):
        kn = jnp.where(is_new, kn_ref[0, n:n + 1, :], kg_ref[:, n, :])
        vn = jnp.where(is_new, vn_ref[0, n:n + 1, :], vg_ref[:, n, :])
        qn = qa_ref[0, n * G:(n + 1) * G, :]
        s = _dot3_nt(qn, kn) * (HEAD_DIM ** -0.5)
        m = jnp.max(s, axis=1, keepdims=True)
        p = jnp.exp(s - m)
        l = jnp.sum(p, axis=1, keepdims=True)
        o_ref[0, n * G:(n + 1) * G, :] = _dot3(p, vn) / l


def dsa_sample(page_table, qi_s, w_s, kin_s, qa_s, kn_s, vn_s, cache_kidx_t, cache_k, cache_v, layer):
    DB, n_pages = page_table.shape
    page = cache_k.shape[2]
    L = n_pages * page + 1
    topk = min(TOPK_MAX, L // 4)
    assert topk % LANES == 0 and topk <= n_pages * page and n_pages % SCORE_PAGES == 0
    blk = lambda shp: pl.BlockSpec((1,) + shp, lambda b, pt: (b,) + (0,) * len(shp))
    any_spec = pl.BlockSpec(memory_space=pl.ANY)
    gs = pltpu.PrefetchScalarGridSpec(
        num_scalar_prefetch=1,
        grid=(DB,),
        in_specs=[blk((IDX_HEADS, IDX_DIM)), blk((IDX_HEADS, 1)), blk((1, IDX_DIM)),
                  blk((ATT_HEADS, HEAD_DIM)), blk((KV_HEADS, HEAD_DIM)), blk((KV_HEADS, HEAD_DIM)),
                  any_spec, any_spec, any_spec],
        out_specs=blk((ATT_HEADS, HEAD_DIM)),
        scratch_shapes=[pltpu.VMEM((n_pages, IDX_DIM, page), F32),
                        pltpu.VMEM((n_pages, page), F32),
                        pltpu.VMEM((SUBLANES, LANES), I32),
                        pltpu.SMEM((SUBLANES, LANES), I32),
                        pltpu.VMEM((topk, KV_HEADS, HEAD_DIM), F32),
                        pltpu.VMEM((topk, KV_HEADS, HEAD_DIM), F32),
                        pltpu.SemaphoreType.DMA, pltpu.SemaphoreType.DMA,
                        pltpu.SemaphoreType.DMA, pltpu.SemaphoreType.DMA],
    )
    return pl.pallas_call(
        functools.partial(_dsa_sample_kernel, layer=layer, n_pages=n_pages, page=page, topk=topk),
        grid_spec=gs,
        out_shape=jax.ShapeDtypeStruct((DB, ATT_HEADS, HEAD_DIM), F32),
        compiler_params=_cparams(("arbitrary",)),
        name="dsa_sample",
    )(page_table, qi_s, w_s, kin_s, qa_s, kn_s, vn_s, cache_kidx_t, cache_k, cache_v)


def _route(lg, ri_ref, rw_ref):
    lane = lax.broadcasted_iota(I32, lg.shape, 1)
    big = jnp.int32(1 << 20)
    neg = -jnp.inf
    l_g = jnp.where(lane < N_GROUPS, lg, neg)
    m_g = jnp.max(l_g, axis=1, keepdims=True)
    g_top = jnp.min(jnp.where(l_g == m_g, lane, big), axis=1, keepdims=True)
    pg_top = 1.0 / jnp.sum(jnp.exp(l_g - m_g), axis=1, keepdims=True)
    e_lane = lane - N_GROUPS
    in_grp = jnp.logical_and(e_lane >= g_top * EXP_PER_GROUP, e_lane < (g_top + 1) * EXP_PER_GROUP)
    l_e = jnp.where(in_grp, lg, neg)
    m1 = jnp.max(l_e, axis=1, keepdims=True)
    i1 = jnp.min(jnp.where(l_e == m1, e_lane, big), axis=1, keepdims=True)
    l_e2 = jnp.where(e_lane == i1, neg, l_e)
    m2 = jnp.max(l_e2, axis=1, keepdims=True)
    i2 = jnp.min(jnp.where(l_e2 == m2, e_lane, big), axis=1, keepdims=True)
    e2 = jnp.exp(m2 - m1)
    w1 = pg_top / (1.0 + e2)
    w2 = pg_top * e2 / (1.0 + e2)
    ri_ref[...] = jnp.where(lane == 0, i1, jnp.where(lane == 1, i2, 0))
    rw_ref[...] = jnp.where(lane == 0, w1, jnp.where(lane == 1, w2, 0.0))


def _mix_kernel(og_ref, at_ref, gg_ref, gd_ref, x_ref, wbg_ref, wbd_ref, wo_ref, g_ref, b_ref,
                wrh_ref, wrl_ref, br_ref, x1_ref, ri_ref, rw_ref, *, alpha):
    bg = jnp.dot(og_ref[...], wbg_ref[...], preferred_element_type=F32)
    bd = jnp.dot(at_ref[...], wbd_ref[...], preferred_element_type=F32)
    merged = jax.nn.sigmoid(gg_ref[...]) * bg + jax.nn.sigmoid(gd_ref[...]) * bd
    mix = jnp.dot(merged.astype(BF16), wo_ref[...], preferred_element_type=F32)
    x1 = _ln(alpha * x_ref[...] + mix, g_ref[...], b_ref[...])
    x1_ref[...] = x1
    xh, xl = _hilo(x1)
    d = lambda p, q: jnp.dot(p, q, preferred_element_type=F32)
    lg = d(xh, wrh_ref[...]) + (d(xh, wrl_ref[...]) + d(xl, wrh_ref[...])) + br_ref[...]
    _route(lg, ri_ref, rw_ref)


def mix_stage(og, att, h, x, wbg, wbd, wo, g, b, wrh, wrl, br, layer, alpha):
    T, D = x.shape
    gw = GLA_HEADS * GLA_DV
    aw = ATT_HEADS * HEAD_DIM
    row = lambda w, c: pl.BlockSpec((TM, w), lambda i: (i, c))
    wsp = lambda r, c: pl.BlockSpec((None, r, c), lambda i: (layer, 0, 0))
    return pl.pallas_call(
        functools.partial(_mix_kernel, alpha=alpha),
        grid=(T // TM,),
        in_specs=[row(gw, 0), row(aw, 0), row(D, C_GG // D), row(D, C_GD // D), row(D, 0),
                  wsp(gw, D), wsp(aw, D), wsp(D, D), wsp(1, D), wsp(1, D),
                  wsp(D, LANES), wsp(D, LANES), wsp(1, LANES)],
        out_specs=[row(D, 0), row(LANES, 0), row(LANES, 0)],
        out_shape=[jax.ShapeDtypeStruct((T, D), F32), jax.ShapeDtypeStruct((T, LANES), I32),
                   jax.ShapeDtypeStruct((T, LANES), F32)],
        compiler_params=_cparams(("parallel",)),
        name="mix_stage",
    )(og, att, h, h, x, wbg, wbd, wo, g, b, wrh, wrl, br)


def _s_merge_kernel(og_ref, at_ref, gg_ref, gd_ref, wbg_ref, wbd_ref, o_ref):
    bg = _dot3(og_ref[...], wbg_ref[...])
    bd = _dot3(at_ref[...], wbd_ref[...])
    o_ref[...] = jax.nn.sigmoid(gg_ref[...]) * bg + jax.nn.sigmoid(gd_ref[...]) * bd


def s_merge(og, att, h, wbg, wbd, layer):
    R = og.shape[0]
    D = wbg.shape[-1]
    gw, aw = og.shape[1], att.shape[1]
    return pl.pallas_call(
        _s_merge_kernel,
        grid=(D // TK_S,),
        in_specs=[pl.BlockSpec((R, gw), lambda j: (0, 0)), pl.BlockSpec((R, aw), lambda j: (0, 0)),
                  pl.BlockSpec((R, TK_S), lambda j: (0, C_GG // TK_S + j)),
                  pl.BlockSpec((R, TK_S), lambda j: (0, C_GD // TK_S + j)),
                  pl.BlockSpec((None, gw, TK_S), lambda j: (layer, 0, j)),
                  pl.BlockSpec((None, aw, TK_S), lambda j: (layer, 0, j))],
        out_specs=pl.BlockSpec((R, TK_S), lambda j: (0, j)),
        out_shape=jax.ShapeDtypeStruct((R, D), F32),
        compiler_params=_cparams(("parallel",)),
        name="s_merge",
    )(og, att, h, h, wbg, wbd)


def _s_outproj_kernel(m_ref, x_ref, wo_ref, g_ref, b_ref, wr_ref, br_ref, x1_ref, ri_ref, rw_ref, acc,
                      *, alpha):
    k = pl.program_id(0)

    @pl.when(k == 0)
    def _():
        acc[...] = jnp.zeros_like(acc)

    acc[...] += _dot3(m_ref[...], wo_ref[...])

    @pl.when(k == pl.num_programs(0) - 1)
    def _():
        x1 = _ln(alpha * x_ref[...] + acc[...], g_ref[...], b_ref[...])
        x1_ref[...] = x1
        _route(_dot3(x1, wr_ref[...]) + br_ref[...], ri_ref, rw_ref)


def s_outproj(merged, x, wo, g, b, w_rt, br, layer, alpha):
    R, D = x.shape
    cst = lambda r, c: pl.BlockSpec((None, r, c), lambda k: (layer, 0, 0))
    full = lambda w: pl.BlockSpec((R, w), lambda k: (0, 0))
    return pl.pallas_call(
        functools.partial(_s_outproj_kernel, alpha=alpha),
        grid=(D // TK_S,),
        in_specs=[pl.BlockSpec((R, TK_S), lambda k: (0, k)), full(D),
                  pl.BlockSpec((None, TK_S, D), lambda k: (layer, k, 0)),
                  cst(1, D), cst(1, D), cst(D, LANES), cst(1, LANES)],
        out_specs=[full(D), full(LANES), full(LANES)],
        out_shape=[jax.ShapeDtypeStruct((R, D), F32), jax.ShapeDtypeStruct((R, LANES), I32),
                   jax.ShapeDtypeStruct((R, LANES), F32)],
        scratch_shapes=[pltpu.VMEM((R, D), F32)],
        compiler_params=_cparams(("arbitrary",)),
        name="s_outproj",
    )(merged, x, wo, g, b, w_rt, br)


def _moe_kernel(te_ref, nv_ref, tok_ref, tok_nx_ref, pair_ref, pair_pv_ref, x_hbm, wg_ref, wu_ref, wd_ref, y_hbm,
                xbuf, ybuf, gsem, ssem, *, precise):
    i = pl.program_id(0)
    n = pl.num_programs(0)
    cur = i % 2
    nv = nv_ref[i]
    nv_next = jnp.where(i + 1 < n, nv_ref[jnp.minimum(i + 1, n - 1)], 0)
    nv_prev = jnp.where(i > 0, nv_ref[jnp.maximum(i - 1, 0)], 0)

    def chunked(count, fn):
        def body(c, carry):
            for u in range(DMA_UNROLL):
                fn(c * DMA_UNROLL + u)
            return carry
        lax.fori_loop(0, (count + DMA_UNROLL - 1) // DMA_UNROLL, body, 0)

    def gcopy(ids_ref, buf, r):
        return pltpu.make_async_copy(x_hbm.at[pl.ds(ids_ref[0, 0, r], 1)], xbuf.at[buf, pl.ds(r, 1)], gsem.at[buf])

    def scopy(ids_ref, buf, r):
        return pltpu.make_async_copy(ybuf.at[buf, pl.ds(r, 1)], y_hbm.at[pl.ds(ids_ref[0, 0, r], 1)], ssem.at[buf])

    @pl.when(i == 0)
    def _():
        xbuf[...] = jnp.zeros_like(xbuf)
        n_real = y_hbm.shape[0] - N_SPARE
        for k in range(N_SPARE // DMA_UNROLL):
            init = pltpu.make_async_copy(xbuf.at[0, pl.ds(0, DMA_UNROLL)],
                                         y_hbm.at[pl.ds(n_real + k * DMA_UNROLL, DMA_UNROLL)], ssem.at[0])
            init.start()
            init.wait()
        chunked(nv, lambda r: gcopy(tok_ref, 0, r).start())

    chunked(nv_next, lambda r: gcopy(tok_nx_ref, 1 - cur, r).start())

    @pl.when(nv > 0)
    def _():
        chunked(nv, lambda r: gcopy(tok_ref, cur, r).wait())
        mm = _dot3 if precise else _bdot
        x = xbuf[cur]
        hg = mm(x, wg_ref[...])
        hu = mm(x, wu_ref[...])
        hid = (hg * jax.nn.sigmoid(hg)) * hu
        ybuf[cur] = mm(hid, wd_ref[...])
        chunked(nv, lambda r: scopy(pair_ref, cur, r).start())

    chunked(nv_prev, lambda r: scopy(pair_pv_ref, 1 - cur, r).wait())

    @pl.when(i == n - 1)
    def _():
        chunked(nv, lambda r: scopy(pair_ref, cur, r).wait())


def moe_stage(x1, tile_e, tile_nv, slot_tok, slot_pair, wg, wu, wd, layer, tm, precise):
    T, D = x1.shape
    n_tiles = tile_e.shape[0]
    F = wg.shape[-1]
    ids = lambda shift: pl.BlockSpec((1, 1, tm), lambda i, te, nv: (jnp.clip(i + shift, 0, n_tiles - 1), 0, 0),
                                     memory_space=pltpu.SMEM)
    gs = pltpu.PrefetchScalarGridSpec(
        num_scalar_prefetch=2,
        grid=(n_tiles,),
        in_specs=[ids(0), ids(1), ids(0), ids(-1),
                  pl.BlockSpec(memory_space=pl.ANY),
                  pl.BlockSpec((None, None, D, F), lambda i, te, nv: (layer, te[i], 0, 0)),
                  pl.BlockSpec((None, None, D, F), lambda i, te, nv: (layer, te[i], 0, 0)),
                  pl.BlockSpec((None, None, F, D), lambda i, te, nv: (layer, te[i], 0, 0))],
        out_specs=pl.BlockSpec(memory_space=pl.ANY),
        scratch_shapes=[pltpu.VMEM((2, tm, D), F32), pltpu.VMEM((2, tm, D), F32),
                        pltpu.SemaphoreType.DMA((2,)), pltpu.SemaphoreType.DMA((2,))],
    )
    return pl.pallas_call(
        functools.partial(_moe_kernel, precise=precise),
        grid_spec=gs,
        out_shape=jax.ShapeDtypeStruct((TOPK_INNER * T + N_SPARE, D), F32),
        compiler_params=_cparams(("arbitrary",)),
        name="moe_stage",
    )(tile_e, tile_nv, slot_tok, slot_tok, slot_pair, slot_pair, x1, wg, wu, wd)


def moe_plan(route_i, tm):
    T = route_i.shape[0]
    P = TOPK_INNER * T
    n_tiles = -(-P // tm) + N_EXPERTS
    eid = route_i[:, :TOPK_INNER].T.reshape(P)
    experts = jnp.arange(N_EXPERTS, dtype=I32)[None, :]
    onehot = (eid[:, None] == experts).astype(I32)
    counts = jnp.sum(onehot, axis=0)
    rank = jnp.sum((jnp.cumsum(onehot, axis=0) - onehot) * onehot, axis=1)
    tiles_per = (counts + tm - 1) // tm
    tile_end = jnp.cumsum(tiles_per)
    tile_start = tile_end - tiles_per
    n_used = tile_end[-1]
    ti = jnp.arange(n_tiles, dtype=I32)
    e_of = lambda t: jnp.sum((tile_end[None, :] <= t[:, None]).astype(I32), axis=1)
    used = ti < n_used
    tile_e = jnp.where(used, e_of(ti), e_of(jnp.broadcast_to(n_used - 1, (n_tiles,))))
    tile_e = jnp.clip(tile_e, 0, N_EXPERTS - 1)
    sel_e = tile_e[:, None] == experts
    local = ti - jnp.sum(jnp.where(sel_e, tile_start[None, :], 0), axis=1)
    cnt_e = jnp.sum(jnp.where(sel_e, counts[None, :], 0), axis=1)
    tile_nv = jnp.where(used, jnp.clip(cnt_e - local * tm, 0, tm), 0).astype(I32)
    slot = jnp.sum(onehot * tile_start[None, :], axis=1) * tm + rank
    s_idx = jnp.arange(n_tiles * tm, dtype=I32)
    spare = P + ((s_idx // tm) % 2) * DMA_UNROLL + s_idx % DMA_UNROLL
    slot_pair = spare.at[slot].set(jnp.arange(P, dtype=I32))
    slot_tok = jnp.where(slot_pair < P, slot_pair % T, 0)
    return tile_e.astype(I32), tile_nv, slot_tok.reshape(n_tiles, 1, tm), slot_pair.reshape(n_tiles, 1, tm)


def _ffn_sum(y0_ref, y1_ref, rw_ref):
    rw = rw_ref[...]
    return rw[:, 0:1] * y0_ref[...] + rw[:, 1:2] * y1_ref[...]


def _ple_kernel(x1_ref, y0_ref, y1_ref, rw_ref, p_ref, g1_ref, b1_ref, wp_ref, wpg_ref, g2_ref, b2_ref,
                x3_ref, x3b_ref, *, alpha):
    ffn = _ffn_sum(y0_ref, y1_ref, rw_ref)
    x2 = _ln(alpha * x1_ref[...] + ffn, g1_ref[...], b1_ref[...])
    ple = (jnp.dot(p_ref[...], wp_ref[...], preferred_element_type=F32)
           * jax.nn.sigmoid(jnp.dot(x2.astype(BF16), wpg_ref[...], preferred_element_type=F32)))
    x3 = _ln(alpha * x2 + ple, g2_ref[...], b2_ref[...])
    x3_ref[...] = x3
    x3b_ref[...] = x3.astype(BF16)


def ple_stage(x1, y, rw, pb, g1, b1, wp, wpg, g2, b2, layer, alpha):
    T, D = x1.shape
    PD = pb.shape[1]
    nt = T // TM
    row = lambda w: pl.BlockSpec((TM, w), lambda i: (i, 0))
    wsp = lambda r, c: pl.BlockSpec((None, r, c), lambda i: (layer, 0, 0))
    return pl.pallas_call(
        functools.partial(_ple_kernel, alpha=alpha),
        grid=(nt,),
        in_specs=[row(D), row(D), pl.BlockSpec((TM, D), lambda i: (nt + i, 0)), row(LANES), row(PD),
                  wsp(1, D), wsp(1, D), wsp(PD, D), wsp(D, D), wsp(1, D), wsp(1, D)],
        out_specs=[row(D), row(D)],
        out_shape=[jax.ShapeDtypeStruct((T, D), F32), jax.ShapeDtypeStruct((T, D), BF16)],
        compiler_params=_cparams(("parallel",)),
        name="ple_stage",
    )(x1, y, y, rw, pb, g1, b1, wp, wpg, g2, b2)


def _s_ple_kernel(x1_ref, y0_ref, y1_ref, rw_ref, p_ref, g1_ref, b1_ref, wp_ref, wpg_ref, g2_ref, b2_ref,
                  x3_ref, x2_s, x2k_s, pl_s, acc, *, alpha):
    k = pl.program_id(0)
    nk = x2k_s.shape[0]

    @pl.when(k == 0)
    def _():
        x2 = _ln(alpha * x1_ref[...] + _ffn_sum(y0_ref, y1_ref, rw_ref), g1_ref[...], b1_ref[...])
        x2_s[...] = x2
        for kk in range(nk):
            x2k_s[kk] = x2[:, kk * TK_S:(kk + 1) * TK_S]
        pl_s[...] = _dot3(p_ref[...], wp_ref[...])
        acc[...] = jnp.zeros_like(acc)

    acc[...] += _dot3(x2k_s[k], wpg_ref[...])

    @pl.when(k == nk - 1)
    def _():
        x3_ref[...] = _ln(alpha * x2_s[...] + pl_s[...] * jax.nn.sigmoid(acc[...]), g2_ref[...], b2_ref[...])


def s_ple(x1, y, rw, p, g1, b1, wp, wpg, g2, b2, layer, alpha):
    R, D = x1.shape
    PD = p.shape[1]
    cst = lambda r, c: pl.BlockSpec((None, r, c), lambda k: (layer, 0, 0))
    full = lambda w: pl.BlockSpec((R, w), lambda k: (0, 0))
    return pl.pallas_call(
        functools.partial(_s_ple_kernel, alpha=alpha),
        grid=(D // TK_S,),
        in_specs=[full(D), full(D), pl.BlockSpec((R, D), lambda k: (1, 0)), full(LANES), full(PD),
                  cst(1, D), cst(1, D), cst(PD, D), pl.BlockSpec((None, TK_S, D), lambda k: (layer, k, 0)),
                  cst(1, D), cst(1, D)],
        out_specs=full(D),
        out_shape=jax.ShapeDtypeStruct((R, D), F32),
        scratch_shapes=[pltpu.VMEM((R, D), F32), pltpu.VMEM((D // TK_S, R, TK_S), F32),
                        pltpu.VMEM((R, D), F32), pltpu.VMEM((R, D), F32)],
        compiler_params=_cparams(("arbitrary",)),
        name="s_ple",
    )(x1, y, y, rw, p, g1, b1, wp, wpg, g2, b2)


def _rope_tables(pos, dh, period):
    rot = dh // 4
    half = rot // 2
    inv = ROPE_THETA ** (-jnp.arange(half, dtype=F32) / half)
    ang = pos.astype(F32)[:, None] * inv[None, :]
    cos, sin = jnp.cos(ang), jnp.sin(ang)
    T = pos.shape[0]
    z = lambda n: jnp.zeros((T, n), F32)
    c = jnp.concatenate([cos, cos, jnp.ones((T, period - rot), F32)], axis=1)
    s1 = jnp.concatenate([-sin, z(period - half)], axis=1)
    s2 = jnp.concatenate([z(half), sin, z(period - rot)], axis=1)
    rep = LANES // period
    return jnp.stack([jnp.tile(c, (1, rep)), jnp.tile(s1, (1, rep)), jnp.tile(s2, (1, rep))])


def kernel(x_prompt, x_sample, cache_k, cache_v, cache_kidx, state_gla, page_table, p_prompt, p_sample, ln_in_g, ln_in_b, w_in, w_decay, b_decay, gla_norm_g, w_branch_gla, w_branch_dsa, w_out, ln_mix_g, ln_mix_b, w_group, b_group, w_router, b_router, w_gate, w_up, w_down, ln_moe_g, ln_moe_b, w_ple, w_ple_gate, ln_ple_g, ln_ple_b):
    B, S, D = x_prompt.shape
    DB, DS, _ = x_sample.shape
    depth = w_in.shape[0]
    n_past = page_table.shape[1] * cache_k.shape[2]
    Tp = B * S
    tm_in = min(TM_IN, Tp)
    assert DS == 1 and DB == SUBLANES and S % GLA_CHUNK == 0 and S % TQ == 0 and Tp % tm_in == 0
    alpha = (2.0 * depth) ** 0.25

    w_in_t = jnp.swapaxes(w_in, 1, 2)
    ckidx_t = jnp.swapaxes(cache_kidx, 2, 3)
    w_small = jnp.concatenate([w_in[..., 6160:6224], w_in[..., 3072:3088], w_in[..., 6224:6240],
                               jnp.zeros((depth, D, LANES - 96), w_in.dtype)], axis=-1)
    w_small_b = w_small.astype(BF16)
    kw = GLA_HEADS * GLA_DK
    wdp32 = jnp.zeros((depth, LANES, kw), F32).at[:, L_GA:L_GA + GLA_RANK, :].set(w_decay)
    wdp = wdp32.astype(BF16)
    bd = b_decay.reshape(depth, 1, kw)
    gn = gla_norm_g.reshape(depth, 1, GLA_DV)
    wbg, wbd, wo = w_branch_gla.astype(BF16), w_branch_dsa.astype(BF16), w_out.astype(BF16)
    w_rt = jnp.concatenate([w_group, w_router, jnp.zeros((depth, D, LANES - N_GROUPS - N_EXPERTS), F32)], axis=-1)
    wrh = w_rt.astype(BF16)
    wrl = (w_rt - wrh.astype(F32)).astype(BF16)
    b_rt = jnp.concatenate([b_group, b_router, jnp.zeros((depth, LANES - N_GROUPS - N_EXPERTS), F32)],
                           axis=-1).reshape(depth, 1, LANES)
    wp, wpg = w_ple.astype(BF16), w_ple_gate.astype(BF16)
    r3 = lambda a: a.reshape(depth, 1, D)
    g_mix, b_mix, g_moe, b_moe, g_ple, b_ple = (r3(a) for a in (ln_mix_g, ln_mix_b, ln_moe_g, ln_moe_b,
                                                                  ln_ple_g, ln_ple_b))

    pos_p = jnp.tile(jnp.arange(S, dtype=I32), B)
    pos_s = jnp.full((DB,), n_past, I32)
    tp128, tp64 = _rope_tables(pos_p, HEAD_DIM, HEAD_DIM), _rope_tables(pos_p, IDX_DIM, IDX_DIM)
    ts128, ts64 = _rope_tables(pos_s, HEAD_DIM, HEAD_DIM), _rope_tables(pos_s, IDX_DIM, IDX_DIM)

    x, xb = ln_in(x_prompt.reshape(Tp, D), ln_in_g, ln_in_b, TM)
    xs, _ = ln_in(x_sample.reshape(DB, D), ln_in_g, ln_in_b, DB)
    outs = {k: [] for k in ("kp", "vp", "kip", "sp", "ks", "vs", "kis", "ss")}
    for l in range(depth):
        h = in_proj_main(xb, w_in_t, l, tm_in)
        hs = in_proj_small(xb, w_small_b, l, tm_in, False)
        ka, va, ki, qab, kab, vab_t, qib, kie, kio = rope_stage(h, hs, tp128, tp64, TM, BF16, True)
        og, s_p = gla_prompt(h, hs, wdp[l], bd[l], gn[l], B, S)
        att = dsa_prompt(qab, qib, hs, kab, vab_t, kie, kio, B, S)
        x1, route_i, route_w = mix_stage(og, att, h, x, wbg, wbd, wo, g_mix, b_mix, wrh, wrl, b_rt, l, alpha)
        y = moe_stage(x1, *moe_plan(route_i, TM), w_gate, w_up, w_down, l, TM, False)
        x, xb = ple_stage(x1, y, route_w, p_prompt[l].reshape(Tp, -1).astype(BF16), g_moe, b_moe, wp, wpg,
                          g_ple, b_ple, l, alpha)
        h_s = s_in_proj_main(xs, w_in_t, l)
        hs_s = in_proj_small(xs, w_small, l, DB, True)
        ka_s, va_s, ki_s, qa_s, _, _, qib_s, _, _ = rope_stage(h_s, hs_s, ts128, ts64, DB, F32, False)
        og_s, s_s = gla_sample(h_s, hs_s, wdp32[l], bd[l], gn[l], state_gla, l)
        at_s = dsa_sample(page_table, qib_s.reshape(DB, IDX_HEADS, IDX_DIM),
                          (hs_s[:, L_IW:L_IW + IDX_HEADS] * (IDX_HEADS ** -0.5)).reshape(DB, IDX_HEADS, 1),
                          ki_s.reshape(DB, 1, IDX_DIM),
                          qa_s.reshape(DB, ATT_HEADS, HEAD_DIM),
                          ka_s.reshape(DB, KV_HEADS, HEAD_DIM), va_s.reshape(DB, KV_HEADS, HEAD_DIM),
                          ckidx_t, cache_k, cache_v, l)
        merged_s = s_merge(og_s, at_s.reshape(DB, -1), h_s, w_branch_gla, w_branch_dsa, l)
        x1_s, ri_s, rw_s = s_outproj(merged_s, xs, w_out, g_mix, b_mix, w_rt, b_rt, l, alpha)
        y_s = moe_stage(x1_s, *moe_plan(ri_s, DB), w_gate, w_up, w_down, l, DB, True)
        xs = s_ple(x1_s, y_s, rw_s, p_sample[l].reshape(DB, -1), g_moe, b_moe, w_ple, w_ple_gate, g_ple, b_ple, l, alpha)
        for k, v in zip(outs, (ka, va, ki, s_p, ka_s, va_s, ki_s, s_s)):
            outs[k].append(v)

    st = {k: jnp.stack(v) for k, v in outs.items()}
    return (x.reshape(B, S, D), xs.reshape(DB, DS, D),
            st["kp"].reshape(depth, B, S, KV_HEADS, HEAD_DIM), st["vp"].reshape(depth, B, S, KV_HEADS, HEAD_DIM),
            st["kip"].reshape(depth, B, S, IDX_DIM), st["sp"],
            st["ks"].reshape(depth, DB, DS, KV_HEADS, HEAD_DIM), st["vs"].reshape(depth, DB, DS, KV_HEADS, HEAD_DIM),
            st["kis"].reshape(depth, DB, DS, IDX_DIM), st["ss"])
```
